```python
import jax, jax.numpy as jnp
from jax import lax
import numpy as np

D_MODEL = 1024
BATCH = 16
SEQ = 256
DEPTH = 1
DEC_BATCH = 2
DEC_SEQ = 4096
PAST_LEN = 512

GRID_W = 64
ML_HEADS = 4
ML_DIM = 1024
ML_HEAD_DIM = ML_DIM // ML_HEADS
SC_DIM = 1024
CONV_W = 3
D_FF = 2816
CHUNK = 128
N_MOD = 9
N_DIR = 2
EPS = 1e-6

Q_OFF = 0
K_OFF = Q_OFF + ML_DIM
V_OFF = K_OFF + ML_DIM
O_OFF = V_OFF + ML_DIM
IG_OFF = O_OFF + ML_DIM
FG_OFF = IG_OFF + N_DIR * ML_HEADS
B_OFF = FG_OFF + N_DIR * ML_HEADS
C_OFF = B_OFF + SC_DIM
X_OFF = C_OFF + SC_DIM
GML_OFF = X_OFF + SC_DIM
GSC_OFF = GML_OFF + D_MODEL
IN_COLS = GSC_OFF + D_MODEL

kernel_name = "hybrid_mlstm_shortconv_diffusion_step"


def rmsnorm(x, g):
    x32 = x.astype(jnp.float32)
    y = x32 * lax.rsqrt(jnp.mean(x32 * x32, axis=-1, keepdims=True) + EPS)
    return (y * g.astype(jnp.float32)).astype(x.dtype)


def modulate(h, shift, scale):
    return h * (1.0 + scale[:, None, :]) + shift[:, None, :]


def swiglu(h, w1, w2):
    a, b = jnp.split(h @ w1, 2, axis=-1)
    return (jax.nn.silu(a) * b) @ w2


def conv3_centred(u, w, b):
    up = jnp.pad(u, [(0, 0)] * (u.ndim - 2) + [(1, 1), (0, 0)])
    return up[..., :-2, :] * w[0] + up[..., 1:-1, :] * w[1] + up[..., 2:, :] * w[2] + b


def mlstm_chunkwise(q, k, v, i_pre, f_pre, C0, n0, m0):
    f32 = jnp.float32
    bsz, nh, t_len, dh = q.shape
    nc = t_len // CHUNK

    def chunks(t):
        t = t.astype(f32).reshape((bsz, nh, nc, CHUNK) + t.shape[3:])
        return jnp.moveaxis(t, 2, 0)

    xs = (chunks(q), chunks(k), chunks(v), chunks(i_pre),
          chunks(jax.nn.log_sigmoid(f_pre.astype(f32))))
    causal = jnp.tril(jnp.ones((CHUNK, CHUNK), dtype=bool))

    def step(carry, inp):
        C, n, m = carry
        qc, kc, vc, ic, lfc = inp
        b = jnp.cumsum(lfc, axis=-1)
        a = b + m[..., None]
        dmat = jnp.where(causal, b[..., :, None] - b[..., None, :] + ic[..., None, :], -jnp.inf)
        m_t = jnp.maximum(a, jnp.max(dmat, axis=-1))
        w_intra = jnp.exp(dmat - m_t[..., None])
        w_inter = jnp.exp(a - m_t)
        s = jnp.einsum('bhtd,bhsd->bhts', qc, kc) * w_intra
        num = (jnp.einsum('bhts,bhse->bhte', s, vc)
               + w_inter[..., None] * jnp.einsum('bhtd,bhde->bhte', qc, C))
        den = jnp.sum(s, axis=-1) + w_inter * jnp.einsum('bhtd,bhd->bht', qc, n)
        h = num / jnp.maximum(jnp.abs(den), jnp.exp(-m_t))[..., None]
        bl = b[..., -1]
        g = bl[..., None] - b + ic
        m_new = jnp.maximum(bl + m, jnp.max(g, axis=-1))
        decay = jnp.exp(bl + m - m_new)
        wk = jnp.exp(g - m_new[..., None])
        C_new = decay[..., None, None] * C + jnp.einsum('bhs,bhsd,bhse->bhde', wk, kc, vc)
        n_new = decay[..., None] * n + jnp.einsum('bhs,bhsd->bhd', wk, kc)
        return (C_new, n_new, m_new), h

    carry0 = (C0.astype(f32), n0.astype(f32), m0.astype(f32))
    (C, n, m), h = lax.scan(step, carry0, xs)
    h = jnp.moveaxis(h, 0, 2).reshape(bsz, nh, t_len, dh)
    return h, C, n, m


def mlstm_bidir(q, k, v, ig, fg, C0, n0, m0):
    hf, Cf, nf, mf = mlstm_chunkwise(q, k, v, ig[:, 0], fg[:, 0], C0[:, 0], n0[:, 0], m0[:, 0])
    rq, rk, rv = jnp.flip(q, axis=2), jnp.flip(k, axis=2), jnp.flip(v, axis=2)
    hb, Cb, nb, mb = mlstm_chunkwise(rq, rk, rv, jnp.flip(ig[:, 1], axis=-1),
                                     jnp.flip(fg[:, 1], axis=-1), C0[:, 1], n0[:, 1], m0[:, 1])
    h = hf + jnp.flip(hb, axis=2)
    return h, jnp.stack([Cf, Cb], axis=1), jnp.stack([nf, nb], axis=1), jnp.stack([mf, mb], axis=1)


def mixer(h, C0, n0, m0, rows, w_in, b_in, conv_w, conv_b, ml_norm, w_ml_out, w_sc_out, w_o):
    bsz, t_len, _ = h.shape
    z = h @ w_in + b_in

    def heads(t):
        return t.reshape(bsz, t_len, ML_HEADS, ML_HEAD_DIM).transpose(0, 2, 1, 3)

    q = heads(z[..., Q_OFF:K_OFF])
    k = heads(z[..., K_OFF:V_OFF]) * (ML_HEAD_DIM ** -0.5)
    v = heads(z[..., V_OFF:O_OFF])
    o = jax.nn.sigmoid(z[..., O_OFF:IG_OFF])
    ig = z[..., IG_OFF:FG_OFF].reshape(bsz, t_len, N_DIR, ML_HEADS).transpose(0, 2, 3, 1)
    fg = z[..., FG_OFF:B_OFF].reshape(bsz, t_len, N_DIR, ML_HEADS).transpose(0, 2, 3, 1)
    hm, C, n, m = mlstm_bidir(q, k, v, ig, fg, C0, n0, m0)
    hm = hm * lax.rsqrt(jnp.mean(hm * hm, axis=-1, keepdims=True) + EPS)
    hm = hm.transpose(0, 2, 1, 3).reshape(bsz, t_len, ML_DIM) * ml_norm
    ml = (o * hm) @ w_ml_out

    bg = z[..., B_OFF:C_OFF]
    cg = z[..., C_OFF:X_OFF]
    xs = z[..., X_OFF:GML_OFF]
    u = cg * xs
    if rows is None:
        uc = conv3_centred(u, conv_w, conv_b)
    else:
        uc = conv3_centred(u.reshape(bsz, rows, GRID_W, SC_DIM), conv_w, conv_b)
        uc = uc.reshape(bsz, t_len, SC_DIM)
    sc = (bg * uc) @ w_sc_out

    y = jax.nn.sigmoid(z[..., GML_OFF:GSC_OFF]) * ml + jax.nn.sigmoid(z[..., GSC_OFF:IN_COLS]) * sc
    return y @ w_o, C, n, m


def trunk_layer(x, cond, C0, n0, m0, rows, w_mod, b_mod, norm_g, ffn1_w1, ffn1_w2, w_in, b_in,
                conv_w, conv_b, ml_norm, w_ml_out, w_sc_out, w_o, ffn2_w1, ffn2_w2):
    mod = jax.nn.silu(cond) @ w_mod + b_mod
    sh1, sc1, g1, shm, scm, gm, sh2, sc2, g2 = jnp.split(mod, N_MOD, axis=-1)
    h = modulate(rmsnorm(x, norm_g[0]), sh1, sc1)
    x = x + 0.5 * g1[:, None, :] * swiglu(h, ffn1_w1, ffn1_w2)
    h = modulate(rmsnorm(x, norm_g[1]), shm, scm)
    y, C, n, m = mixer(h, C0, n0, m0, rows, w_in, b_in, conv_w, conv_b, ml_norm,
                       w_ml_out, w_sc_out, w_o)
    x = x + gm[:, None, :] * y
    h = modulate(rmsnorm(x, norm_g[2]), sh2, sc2)
    x = x + 0.5 * g2[:, None, :] * swiglu(h, ffn2_w1, ffn2_w2)
    return x, C, n, m


def setup_inputs(seed: int = 0) -> dict:
    key = jax.random.key(seed)
    ks = jax.random.split(key, 24)
    f32 = jnp.float32
    D = D_MODEL

    def nrm(k, shape, s):
        return jax.random.normal(k, shape, f32) * s

    fbias = jnp.tile(jnp.linspace(3.0, 6.0, ML_HEADS, dtype=f32), N_DIR)
    return {
        'x_prompt': nrm(ks[0], (BATCH, SEQ, D), 1.0),
        'x_sample': nrm(ks[1], (DEC_BATCH, DEC_SEQ, D), 1.0),
        'state_C': nrm(ks[2], (DEC_BATCH, DEPTH, N_DIR, ML_HEADS, ML_HEAD_DIM, ML_HEAD_DIM), 0.5),
        'state_n': nrm(ks[3], (DEC_BATCH, DEPTH, N_DIR, ML_HEADS, ML_HEAD_DIM), 0.5),
        'state_m': nrm(ks[4], (DEC_BATCH, DEPTH, N_DIR, ML_HEADS), 1.0),
        'c': nrm(ks[5], (DEC_BATCH, D), 1.0),
        'c_ctx': nrm(ks[6], (D,), 1.0),
        'w_mod': nrm(ks[7], (DEPTH, D, N_MOD * D), 0.5 * D ** -0.5),
        'b_mod': nrm(ks[8], (DEPTH, N_MOD * D), 0.02),
        'norm_g': 1.0 + nrm(ks[9], (DEPTH, 3, D), 0.02),
        'ffn1_w1': nrm(ks[10], (DEPTH, D, 2 * D_FF), D ** -0.5),
        'ffn1_w2': nrm(ks[11], (DEPTH, D_FF, D), D_FF ** -0.5),
        'w_in': nrm(ks[12], (DEPTH, D, IN_COLS), D ** -0.5),
        'b_in': nrm(ks[13], (DEPTH, IN_COLS), 0.02).at[:, FG_OFF:B_OFF].add(fbias),
        'conv_w': nrm(ks[14], (DEPTH, CONV_W, SC_DIM), CONV_W ** -0.5),
        'conv_b': nrm(ks[15], (DEPTH, SC_DIM), 0.02),
        'ml_norm': 1.0 + nrm(ks[16], (DEPTH, ML_DIM), 0.02),
        'w_ml_out': nrm(ks[17], (DEPTH, ML_DIM, D), ML_DIM ** -0.5),
        'w_sc_out': nrm(ks[18], (DEPTH, SC_DIM, D), SC_DIM ** -0.5),
        'w_o': nrm(ks[19], (DEPTH, D, D), D ** -0.5),
        'ffn2_w1': nrm(ks[20], (DEPTH, D, 2 * D_FF), D ** -0.5),
        'ffn2_w2': nrm(ks[21], (DEPTH, D_FF, D), D_FF ** -0.5),
        'final_norm': 1.0 + nrm(ks[22], (D,), 0.02),
    }


def reference(x_prompt, x_sample, state_C, state_n, state_m, c, c_ctx, w_mod, b_mod, norm_g,
              ffn1_w1, ffn1_w2, w_in, b_in, conv_w, conv_b, ml_norm, w_ml_out, w_sc_out, w_o,
              ffn2_w1, ffn2_w2, final_norm):
    f32 = jnp.float32
    bp = x_prompt.shape[0]
    rows = x_sample.shape[1] // GRID_W
    zC = jnp.zeros((bp, N_DIR, ML_HEADS, ML_HEAD_DIM, ML_HEAD_DIM), f32)
    zn = jnp.zeros((bp, N_DIR, ML_HEADS, ML_HEAD_DIM), f32)
    zm = jnp.zeros((bp, N_DIR, ML_HEADS), f32)
    xp, xs = x_prompt, x_sample
    Cs, ns, ms = [], [], []
    for l in range(DEPTH):
        lw = (w_mod[l], b_mod[l], norm_g[l], ffn1_w1[l], ffn1_w2[l], w_in[l], b_in[l],
              conv_w[l], conv_b[l], ml_norm[l], w_ml_out[l], w_sc_out[l], w_o[l],
              ffn2_w1[l], ffn2_w2[l])
        xp, Cl, nl, ml_ = trunk_layer(xp, c_ctx[None, :], zC, zn, zm, None, *lw)
        Cs.append(Cl)
        ns.append(nl)
        ms.append(ml_)
        xs, _, _, _ = trunk_layer(xs, c, state_C[:, l], state_n[:, l], state_m[:, l], rows, *lw)
    y_prompt = rmsnorm(xp, final_norm)
    y_sample = rmsnorm(xs, final_norm)
    new_state_C = jnp.stack(Cs, axis=1)
    new_state_n = jnp.stack(ns, axis=1)
    new_state_m = jnp.stack(ms, axis=1)
    return (y_prompt, y_sample, new_state_C, new_state_n, new_state_m)
```

```python
import functools

import jax
import jax.numpy as jnp
from jax import lax
from jax.experimental import pallas as pl
from jax.experimental.pallas import tpu as pltpu

F32 = jnp.float32
BF16 = jnp.bfloat16

NH = 4
N_DIR = 2
N_MOD = 9
GRID_W = 64
EPS = 1e-6
LANES = 128
SUBLANES = 8
CHUNK = 128
VMEM_LIMIT = 56 * 1024 * 1024

TM_FFN = 512
TM_MIX = 256
TM_IN = 1024
TN_IN = 1024


def _params(n_axes):
    return pltpu.CompilerParams(dimension_semantics=("arbitrary",) * n_axes,
                                vmem_limit_bytes=VMEM_LIMIT)


def _resident(shape):
    zeros = (0,) * len(shape)
    return pl.BlockSpec(shape, lambda *_: zeros, pipeline_mode=pl.Buffered(1))


def _rms(x, g):
    return x * lax.rsqrt(jnp.mean(x * x, axis=-1, keepdims=True) + EPS) * g


def _bdot(a, b):
    return jnp.dot(a, b, preferred_element_type=F32)


def _mod_kernel(condT_ref, w_ref, b_ref, o_ref, *, n_cond):
    s = jax.nn.silu(condT_ref[...])
    w = w_ref[...]
    rows = [jnp.sum(w * s[:, r:r + 1], axis=0, keepdims=True) for r in range(n_cond)]
    rows.append(jnp.zeros((SUBLANES - n_cond, w.shape[1]), F32))
    o_ref[...] = jnp.concatenate(rows, axis=0) + b_ref[...]


def _mod_call(condT, w_mod, b_mod, n_cond):
    d, n = w_mod.shape
    tn = 1024
    return pl.pallas_call(
        functools.partial(_mod_kernel, n_cond=n_cond),
        grid=(n // tn,),
        in_specs=[pl.BlockSpec((d, SUBLANES), lambda j: (0, 0)),
                  pl.BlockSpec((d, tn), lambda j: (0, j)),
                  pl.BlockSpec((1, tn), lambda j: (0, j))],
        out_specs=pl.BlockSpec((SUBLANES, tn), lambda j: (0, j)),
        out_shape=jax.ShapeDtypeStruct((SUBLANES, n), F32),
        compiler_params=_params(1),
        name="mod",
    )(condT, w_mod, b_mod)


def _swiglu_residual(x, mod, g_norm, w1_ref, w2_ref, chunks, row0):
    d_ff = w2_ref.shape[0]
    h = (_rms(x, g_norm) * (1.0 + mod[row0 + 1:row0 + 2]) + mod[row0:row0 + 1]).astype(BF16)
    acc = None
    for lo, hi in chunks:
        a = _bdot(h, w1_ref[:, lo:hi])
        b = _bdot(h, w1_ref[:, d_ff + lo:d_ff + hi])
        part = _bdot((jax.nn.silu(a) * b).astype(BF16), w2_ref[lo:hi, :])
        acc = part if acc is None else acc + part
    return x + (0.5 * mod[row0 + 2:row0 + 3]) * acc


def _ffn1_kernel(xp_ref, xs_ref, mod_ref, g_ref, w1_ref, w2_ref, x1_ref, h2_ref, *,
                 chunks, n_ctx_blocks):
    is_ctx = pl.program_id(0) < n_ctx_blocks
    x = jnp.where(is_ctx, xp_ref[...], xs_ref[...])
    mod = mod_ref[0]
    x1 = _swiglu_residual(x, mod, g_ref[0:1], w1_ref, w2_ref, chunks, 0)
    x1_ref[...] = x1
    h2_ref[...] = (_rms(x1, g_ref[1:2]) * (1.0 + mod[4:5]) + mod[3:4]).astype(BF16)


def _ffn2_kernel(x_ref, mod_ref, g_ref, fin_ref, w1_ref, w2_ref, yp_ref, ys_ref, *,
                 chunks, n_ctx_blocks):
    i = pl.program_id(0)
    x3 = _swiglu_residual(x_ref[...], mod_ref[0], g_ref[2:3], w1_ref, w2_ref, chunks, 6)
    y = _rms(x3, fin_ref[...])

    @pl.when(i < n_ctx_blocks)
    def _():
        yp_ref[...] = y

    @pl.when(i >= n_ctx_blocks)
    def _():
        ys_ref[...] = y


def _ff_chunks(d_ff):
    tile = 256
    half = max(tile, (d_ff // 2 + tile - 1) // tile * tile)
    return ((0, half), (half, d_ff)) if half < d_ff else ((0, d_ff),)


def _inproj_kernel(h_ref, w_ref, b_ref, wg_ref, bg_ref, z_ref, gates_ref):
    h = h_ref[...]
    z_ref[...] = (_bdot(h, w_ref[...]) + b_ref[...]).astype(BF16)

    @pl.when(pl.program_id(1) == 0)
    def _():
        gates_ref[...] = _bdot(h, wg_ref[...]) + bg_ref[...]


def _mlstm_kernel(*refs, nc, has_init, write_state):
    it = iter(refs)
    q_ref, k_ref, v_ref, gr_ref = next(it), next(it), next(it), next(it)
    if has_init:
        c0_ref, n0_ref, m0_ref = next(it), next(it), next(it)
    h_ref = next(it)
    if write_state:
        cout_ref, nout_ref, mout_ref = next(it), next(it), next(it)
    c_sc, n_sc, m_sc, b_sc, tot_sc = next(it), next(it), next(it), next(it), next(it)

    L = CHUNK
    b_idx, h_idx, d = pl.program_id(0), pl.program_id(1), pl.program_id(2)
    fwd = d == 0

    @pl.when(fwd)
    def _():
        h_ref[...] = jnp.zeros(h_ref.shape, F32)

    lf = jax.nn.log_sigmoid(gr_ref[1, 0, 0, 0])
    lane = lax.broadcasted_iota(jnp.int32, lf.shape, 1)
    bf = lf
    sh = 1
    while sh < L:
        bf = bf + jnp.where(lane >= sh, pltpu.roll(bf, sh, axis=1), 0.0)
        sh *= 2
    tot = jnp.broadcast_to(bf[:, L - 1:L], lf.shape)
    b_sc[...] = jnp.where(fwd, bf, tot - bf + lf)
    tot_sc[...] = tot

    if has_init:
        c_sc[...] = c0_ref[0, 0, 0, 0]
        n_sc[...] = n0_ref[0, 0, 0, pl.ds(h_idx, 1), :]
        m_sc[...] = jnp.full(m_sc.shape, m0_ref[(b_idx * N_DIR + d) * NH + h_idx], F32)
    else:
        c_sc[...] = jnp.zeros(c_sc.shape, F32)
        n_sc[...] = jnp.zeros(n_sc.shape, F32)
        m_sc[...] = jnp.zeros(m_sc.shape, F32)

    row = lax.broadcasted_iota(jnp.int32, (L, L), 0)
    col = lax.broadcasted_iota(jnp.int32, (L, L), 1)
    eye = row == col
    tri = (row - col) * jnp.where(fwd, 1, -1) >= 0

    def chunk_step(i, carry):
        c = jnp.where(fwd, i, nc - 1 - i)
        r0 = pl.multiple_of(c * L, L)
        qc = q_ref[pl.ds(r0, L), :]
        kc = k_ref[pl.ds(r0, L), :]
        vc = v_ref[pl.ds(r0, L), :]
        ig_r = gr_ref[0, 0, 0, 0, pl.ds(c, 1), :]
        b_r = b_sc[pl.ds(c, 1), :]
        bl = tot_sc[pl.ds(c, 1), 0:1]
        m = m_sc[0:1, 0:1]
        b_c = jnp.sum(jnp.where(eye, b_r, 0.0), axis=1, keepdims=True)
        ig_c = jnp.sum(jnp.where(eye, ig_r, 0.0), axis=1, keepdims=True)

        dmat = jnp.where(tri, b_c + (ig_r - b_r), -jnp.inf)
        a_c = b_c + m
        m_t = jnp.maximum(a_c, jnp.max(dmat, axis=1, keepdims=True))
        w_inter = jnp.exp(a_c - m_t)
        s = lax.dot_general(qc, kc, (((1,), (1,)), ((), ())),
                            preferred_element_type=F32) * jnp.exp(dmat - m_t)
        num = _bdot(s.astype(BF16), vc) + w_inter * _bdot(qc, c_sc[...].astype(BF16))
        qn = jnp.sum(qc.astype(F32) * n_sc[...], axis=1, keepdims=True)
        den = jnp.sum(s, axis=1, keepdims=True) + w_inter * qn
        h_ref[pl.ds(r0, L), :] += num / jnp.maximum(jnp.abs(den), jnp.exp(-m_t))

        g_c = bl - b_c + ig_c
        m_new = jnp.maximum(bl + m, jnp.max(g_c, axis=0, keepdims=True))
        decay = jnp.exp(bl + m - m_new)
        kw = kc.astype(F32) * jnp.exp(g_c - m_new)
        c_sc[...] = decay * c_sc[...] + lax.dot_general(
            kw.astype(BF16), vc, (((0,), (0,)), ((), ())), preferred_element_type=F32)
        n_sc[...] = decay * n_sc[...] + jnp.sum(kw, axis=0, keepdims=True)
        m_sc[...] = jnp.broadcast_to(m_new, m_sc.shape)
        return carry

    lax.fori_loop(0, nc, chunk_step, 0)

    if write_state:
        cout_ref[0, 0, 0, 0] = c_sc[...]
        nout_ref[0, 0, 0] = n_sc[...]
        mout_ref[0, 0, 0] = m_sc[...]


def _mlstm_call(z, gates_rows, state, *, n_seq, t_len, row_block0, dh):
    nc = t_len // CHUNK
    rows = gates_rows.shape[-2]
    has_init = state is not None
    write_state = not has_init
    kern = functools.partial(_mlstm_kernel, nc=nc, has_init=has_init, write_state=write_state)

    def qkv_spec(which):
        return pl.BlockSpec((t_len, dh), lambda b, h, d: (row_block0 + b, which * NH + h))

    in_specs = [qkv_spec(0), qkv_spec(1), qkv_spec(2),
                pl.BlockSpec((2, 1, 1, 1, rows, CHUNK), lambda b, h, d: (0, d, h, b, 0, 0))]
    args = [z, z, z, gates_rows]
    if has_init:
        c0, n0, m0 = state
        in_specs += [pl.BlockSpec((1, 1, 1, 1, dh, dh), lambda b, h, d: (b, 0, d, h, 0, 0)),
                     pl.BlockSpec((1, 1, 1, NH, dh), lambda b, h, d: (b, 0, d, 0, 0)),
                     pl.BlockSpec(memory_space=pltpu.SMEM)]
        args += [c0, n0, m0.reshape(-1)]
    out_specs = [pl.BlockSpec((t_len, dh), lambda b, h, d: (b, h))]
    out_shape = [jax.ShapeDtypeStruct((n_seq * t_len, NH * dh), F32)]
    if write_state:
        out_specs += [pl.BlockSpec((1, 1, 1, 1, dh, dh), lambda b, h, d: (b, 0, d, h, 0, 0)),
                      pl.BlockSpec((1, 1, 1, 1, dh), lambda b, h, d: (b, d, h, 0, 0)),
                      pl.BlockSpec((1, 1, 1, 1, LANES), lambda b, h, d: (b, d, h, 0, 0))]
        out_shape += [jax.ShapeDtypeStruct((n_seq, 1, N_DIR, NH, dh, dh), F32),
                      jax.ShapeDtypeStruct((n_seq, N_DIR, NH, 1, dh), F32),
                      jax.ShapeDtypeStruct((n_seq, N_DIR, NH, 1, LANES), F32)]
    return pl.pallas_call(
        kern,
        grid=(n_seq, NH, N_DIR),
        in_specs=in_specs,
        out_specs=out_specs,
        out_shape=out_shape,
        scratch_shapes=[pltpu.VMEM((dh, dh), F32), pltpu.VMEM((1, dh), F32),
                        pltpu.VMEM((1, LANES), F32), pltpu.VMEM((rows, CHUNK), F32),
                        pltpu.VMEM((rows, CHUNK), F32)],
        compiler_params=_params(3),
        name="mlstm_lat" if has_init else "mlstm_ctx",
    )(*args)


def _mixout_kernel(hmc_ref, hml_ref, zo_ref, zb_ref, zc_ref, zx_ref, zgm_ref, zgs_ref, x1_ref,
                   mod_ref, mln_ref, cw_ref, cb_ref, wml_ref, wsc_ref, wo_ref, x2_ref, *,
                   n_ctx_blocks, ctx_seg, dh):
    is_ctx = pl.program_id(0) < n_ctx_blocks
    tm = x1_ref.shape[0]
    hm = jnp.where(is_ctx, hmc_ref[...], hml_ref[...])
    heads = []
    for hh in range(NH):
        x = hm[:, hh * dh:(hh + 1) * dh]
        heads.append(x * lax.rsqrt(jnp.mean(x * x, axis=-1, keepdims=True) + EPS))
    hn = jnp.concatenate(heads, axis=1) * mln_ref[...]
    o = jax.nn.sigmoid(zo_ref[...].astype(F32))
    ml = _bdot((o * hn).astype(BF16), wml_ref[...])

    u = zc_ref[...].astype(F32) * zx_ref[...].astype(F32)
    t = lax.broadcasted_iota(jnp.int32, (tm, 1), 0)
    pos = jnp.where(is_ctx, t % ctx_seg, t % GRID_W)
    has_prev = pos != 0
    has_next = pos != jnp.where(is_ctx, ctx_seg - 1, GRID_W - 1)
    u_prev = jnp.where(has_prev, pltpu.roll(u, 1, axis=0), 0.0)
    u_next = jnp.where(has_next, pltpu.roll(u, tm - 1, axis=0), 0.0)
    uc = u_prev * cw_ref[0:1] + u * cw_ref[1:2] + u_next * cw_ref[2:3] + cb_ref[...]
    sc = _bdot((zb_ref[...].astype(F32) * uc).astype(BF16), wsc_ref[...])

    y = (jax.nn.sigmoid(zgm_ref[...].astype(F32)) * ml
         + jax.nn.sigmoid(zgs_ref[...].astype(F32)) * sc)
    x2_ref[...] = x1_ref[...] + mod_ref[0, 5:6] * _bdot(y.astype(BF16), wo_ref[...])


def kernel(x_prompt, x_sample, state_C, state_n, state_m, c, c_ctx, w_mod, b_mod, norm_g,
           ffn1_w1, ffn1_w2, w_in, b_in, conv_w, conv_b, ml_norm, w_ml_out, w_sc_out, w_o,
           ffn2_w1, ffn2_w2, final_norm):
    assert w_mod.shape[0] == 1, "single trunk layer only"
    bp, tp, dm = x_prompt.shape
    bs, ts, _ = x_sample.shape
    n_ctx, n_lat = bp * tp, bs * ts
    n_tok = n_ctx + n_lat
    d_ff = ffn1_w2.shape[1]
    ml_dim = w_ml_out.shape[1]
    sc_dim = w_sc_out.shape[1]
    dh = ml_dim // NH
    assert ml_dim == dm and sc_dim == dm and dh % 256 == 0
    assert n_ctx % TM_FFN == 0 and ts % TM_FFN == 0 and n_ctx % ts == 0 and n_tok % TM_IN == 0
    assert tp % TM_MIX == 0 and TM_MIX % GRID_W == 0 and tp % CHUNK == 0 and ts % CHUNK == 0
    n_cond = 1 + bs

    cond = jnp.concatenate([c_ctx[None, :], c], axis=0)
    condT = jnp.pad(cond.T, ((0, 0), (0, SUBLANES - n_cond)))
    mod = _mod_call(condT, w_mod[0], b_mod[0][None, :], n_cond).reshape(SUBLANES, N_MOD, dm)

    def mod_spec(tm):
        ncb, per_seq = n_ctx // tm, ts // tm
        return pl.BlockSpec((1, N_MOD, dm),
                            lambda i, *_: (jnp.where(i < ncb, 0, 1 + (i - ncb) // per_seq), 0, 0))

    ncb = n_ctx // TM_FFN
    nb = n_tok // TM_FFN
    chunks = _ff_chunks(d_ff)
    tok_spec = pl.BlockSpec((TM_FFN, dm), lambda i: (i, 0))
    x1, h2 = pl.pallas_call(
        functools.partial(_ffn1_kernel, chunks=chunks, n_ctx_blocks=ncb),
        grid=(nb,),
        in_specs=[pl.BlockSpec((TM_FFN, dm), lambda i: (jnp.minimum(i, ncb - 1), 0)),
                  pl.BlockSpec((TM_FFN, dm), lambda i: (jnp.maximum(i - ncb, 0), 0)),
                  mod_spec(TM_FFN), _resident((3, dm)),
                  _resident((dm, 2 * d_ff)), _resident((d_ff, dm))],
        out_specs=[tok_spec, tok_spec],
        out_shape=[jax.ShapeDtypeStruct((n_tok, dm), F32), jax.ShapeDtypeStruct((n_tok, dm), BF16)],
        compiler_params=_params(1),
        name="ffn1",
    )(x_prompt.reshape(n_ctx, dm), x_sample.reshape(n_lat, dm), mod, norm_g[0],
      ffn1_w1[0].astype(BF16), ffn1_w2[0].astype(BF16))

    w = w_in[0]
    b = b_in[0]
    kscale = dh ** -0.5
    gate_lo, gate_hi = 4 * ml_dim, 4 * ml_dim + 2 * N_DIR * NH
    w_main = jnp.concatenate([w[:, :ml_dim], w[:, ml_dim:2 * ml_dim] * kscale,
                              w[:, 2 * ml_dim:gate_lo], w[:, gate_hi:]], axis=1).astype(BF16)
    b_main = jnp.concatenate([b[:ml_dim], b[ml_dim:2 * ml_dim] * kscale,
                              b[2 * ml_dim:gate_lo], b[gate_hi:]])[None, :]
    n_gate = gate_hi - gate_lo
    w_gate = jnp.pad(w[:, gate_lo:gate_hi], ((0, 0), (0, LANES - n_gate))).astype(BF16)
    b_gate = jnp.pad(b[gate_lo:gate_hi], (0, LANES - n_gate))[None, :]
    n_main = w_main.shape[1]
    z, gates = pl.pallas_call(
        _inproj_kernel,
        grid=(n_tok // TM_IN, n_main // TN_IN),
        in_specs=[pl.BlockSpec((TM_IN, dm), lambda i, j: (i, 0)),
                  pl.BlockSpec((dm, TN_IN), lambda i, j: (0, j)),
                  pl.BlockSpec((1, TN_IN), lambda i, j: (0, j)),
                  _resident((dm, LANES)), _resident((1, LANES))],
        out_specs=[pl.BlockSpec((TM_IN, TN_IN), lambda i, j: (i, j)),
                   pl.BlockSpec((TM_IN, LANES), lambda i, j: (i, 0))],
        out_shape=[jax.ShapeDtypeStruct((n_tok, n_main), BF16),
                   jax.ShapeDtypeStruct((n_tok, LANES), F32)],
        compiler_params=_params(2),
        name="inproj",
    )(h2, w_main, b_main, w_gate, b_gate)

    g_rows = gates[:, :n_gate].T.reshape(2, N_DIR, NH, n_tok)

    def chunk_rows(g, n_seq, t_len):
        g = g.reshape(2, N_DIR, NH, n_seq, t_len // CHUNK, CHUNK)
        pad = max(SUBLANES, t_len // CHUNK) - t_len // CHUNK
        return jnp.pad(g, ((0, 0),) * 4 + ((0, pad), (0, 0)))

    hm_ctx, new_c, new_n, new_m = _mlstm_call(
        z, chunk_rows(g_rows[..., :n_ctx], bp, tp), None,
        n_seq=bp, t_len=tp, row_block0=0, dh=dh)
    (hm_lat,) = _mlstm_call(
        z, chunk_rows(g_rows[..., n_ctx:], bs, ts), (state_C, state_n, state_m),
        n_seq=bs, t_len=ts, row_block0=n_ctx // ts, dh=dh)

    ncb_m = n_ctx // TM_MIX
    tokm = pl.BlockSpec((TM_MIX, dm), lambda i: (i, 0))

    def zcol(j):
        return pl.BlockSpec((TM_MIX, dm), lambda i: (i, j))

    x2 = pl.pallas_call(
        functools.partial(_mixout_kernel, n_ctx_blocks=ncb_m, ctx_seg=tp, dh=dh),
        grid=(n_tok // TM_MIX,),
        in_specs=[pl.BlockSpec((TM_MIX, dm), lambda i: (jnp.minimum(i, ncb_m - 1), 0)),
                  pl.BlockSpec((TM_MIX, dm), lambda i: (jnp.maximum(i - ncb_m, 0), 0)),
                  zcol(3), zcol(4), zcol(5), zcol(6), zcol(7), zcol(8), tokm,
                  mod_spec(TM_MIX), _resident((1, dm)), _resident((3, dm)), _resident((1, dm)),
                  _resident((dm, dm)), _resident((dm, dm)), _resident((dm, dm))],
        out_specs=tokm,
        out_shape=jax.ShapeDtypeStruct((n_tok, dm), F32),
        compiler_params=_params(1),
        name="mixout",
    )(hm_ctx, hm_lat, z, z, z, z, z, z, x1, mod, ml_norm[0][None, :], conv_w[0],
      conv_b[0][None, :], w_ml_out[0].astype(BF16), w_sc_out[0].astype(BF16), w_o[0].astype(BF16))

    y_p, y_s = pl.pallas_call(
        functools.partial(_ffn2_kernel, chunks=chunks, n_ctx_blocks=ncb),
        grid=(nb,),
        in_specs=[tok_spec, mod_spec(TM_FFN), _resident((3, dm)), _resident((1, dm)),
                  _resident((dm, 2 * d_ff)), _resident((d_ff, dm))],
        out_specs=[pl.BlockSpec((TM_FFN, dm), lambda i: (jnp.minimum(i, ncb - 1), 0)),
                   pl.BlockSpec((TM_FFN, dm), lambda i: (jnp.maximum(i - ncb, 0), 0))],
        out_shape=[jax.ShapeDtypeStruct((n_ctx, dm), F32), jax.ShapeDtypeStruct((n_lat, dm), F32)],
        compiler_params=_params(1),
        name="ffn2",
    )(x2, mod, norm_g[0], final_norm[None, :], ffn2_w1[0].astype(BF16), ffn2_w2[0].astype(BF16))

    return (y_p.reshape(bp, tp, dm), y_s.reshape(bs, ts, dm), new_c,
            new_n.reshape(bp, 1, N_DIR, NH, dh), new_m[..., 0, 0].reshape(bp, 1, N_DIR, NH))
```

```python
import functools

import jax
import jax.numpy as jnp
from jax import lax
from jax.experimental import pallas as pl
from jax.experimental.pallas import tpu as pltpu

F32 = jnp.float32
BF16 = jnp.bfloat16

NH = 4
N_DIR = 2
N_MOD = 9
GRID_W = 64
EPS = 1e-6
LANES = 128
SUBLANES = 8
VMEM_LIMIT = 56 * 1024 * 1024

TM_FFN = 512
TM_MIX = 256
TM_IN = 1024
TN_IN = 1024
CHUNK = 256
CTX_SEQS = 8
ZQ, ZV, ZO, ZB, ZC, ZX, ZGM, ZGS = range(8)
SIG_GROUPS = (ZO, ZGM, ZGS)


def _params(n_axes):
    return pltpu.CompilerParams(dimension_semantics=("arbitrary",) * n_axes,
                                vmem_limit_bytes=VMEM_LIMIT)


def _resident(shape):
    zeros = (0,) * len(shape)
    return pl.BlockSpec(shape, lambda *_: zeros, pipeline_mode=pl.Buffered(1))


def _rms(x, g):
    return x * lax.rsqrt(jnp.mean(x * x, axis=-1, keepdims=True) + EPS) * g


def _bdot(a, b):
    return jnp.dot(a, b, preferred_element_type=F32)


def _mod_kernel(condT_ref, w_ref, b_ref, o_ref, *, n_cond):
    s = jax.nn.silu(condT_ref[...])
    w = w_ref[...]
    rows = [jnp.sum(w * s[:, r:r + 1], axis=0, keepdims=True) for r in range(n_cond)]
    rows.append(jnp.zeros((SUBLANES - n_cond, w.shape[1]), F32))
    o_ref[...] = jnp.concatenate(rows, axis=0) + b_ref[...]


def _mod_call(condT, w_mod, b_mod, n_cond):
    d, n = w_mod.shape
    tn = 1024
    return pl.pallas_call(
        functools.partial(_mod_kernel, n_cond=n_cond),
        grid=(n // tn,),
        in_specs=[pl.BlockSpec((d, SUBLANES), lambda j: (0, 0)),
                  pl.BlockSpec((d, tn), lambda j: (0, j)),
                  pl.BlockSpec((1, tn), lambda j: (0, j))],
        out_specs=pl.BlockSpec((SUBLANES, tn), lambda j: (0, j)),
        out_shape=jax.ShapeDtypeStruct((SUBLANES, n), F32),
        compiler_params=_params(1),
        name="mod",
    )(condT, w_mod, b_mod)


def _swiglu_residual(x, mod, g_norm, w1_ref, w2_ref, chunks, row0):
    d_ff = w2_ref.shape[0]
    h = (_rms(x, g_norm) * (1.0 + mod[row0 + 1:row0 + 2]) + mod[row0:row0 + 1]).astype(BF16)
    acc = None
    for lo, hi in chunks:
        a = _bdot(h, w1_ref[:, lo:hi])
        b = _bdot(h, w1_ref[:, d_ff + lo:d_ff + hi])
        part = _bdot((jax.nn.silu(a) * b).astype(BF16), w2_ref[lo:hi, :])
        acc = part if acc is None else acc + part
    return x + (0.5 * mod[row0 + 2:row0 + 3]) * acc


def _ffn1_kernel(xp_ref, xs_ref, mod_ref, g_ref, w1_ref, w2_ref, x1_ref, h2_ref, *,
                 chunks, n_ctx_blocks):
    is_ctx = pl.program_id(0) < n_ctx_blocks
    x = jnp.where(is_ctx, xp_ref[...], xs_ref[...])
    mod = mod_ref[0]
    x1 = _swiglu_residual(x, mod, g_ref[0:1], w1_ref, w2_ref, chunks, 0)
    x1_ref[...] = x1
    h2_ref[...] = (_rms(x1, g_ref[1:2]) * (1.0 + mod[4:5]) + mod[3:4]).astype(BF16)


def _ffn2_kernel(x_ref, mod_ref, g_ref, fin_ref, w1_ref, w2_ref, yp_ref, ys_ref, *,
                 chunks, n_ctx_blocks):
    i = pl.program_id(0)
    x3 = _swiglu_residual(x_ref[...], mod_ref[0], g_ref[2:3], w1_ref, w2_ref, chunks, 6)
    y = _rms(x3, fin_ref[...])

    @pl.when(i < n_ctx_blocks)
    def _():
        yp_ref[...] = y

    @pl.when(i >= n_ctx_blocks)
    def _():
        ys_ref[...] = y


def _ff_chunks(d_ff):
    tile = 256
    half = max(tile, (d_ff // 2 + tile - 1) // tile * tile)
    return ((0, half), (half, d_ff)) if half < d_ff else ((0, d_ff),)


def _inproj_kernel(h_ref, wa_ref, wb_ref, b_ref, z_ref, w_sc, *, n_lead, shift):
    j, i = pl.program_id(0), pl.program_id(1)

    @pl.when(jnp.logical_and(i == 0, j < n_lead))
    def _():
        w_sc[...] = wa_ref[...].astype(BF16)

    @pl.when(jnp.logical_and(i == 0, j >= n_lead))
    def _():
        w_sc[...] = jnp.concatenate([wa_ref[:, shift:], wb_ref[:, :shift]], axis=1).astype(BF16)

    acc = _bdot(h_ref[...], w_sc[...]) + b_ref[...]
    is_sig = functools.reduce(jnp.logical_or, [j == g for g in SIG_GROUPS])

    @pl.when(is_sig)
    def _():
        z_ref[...] = jax.nn.sigmoid(acc).astype(BF16)

    @pl.when(jnp.logical_not(is_sig))
    def _():
        z_ref[...] = acc.astype(BF16)


def _kproj_kernel(h_ref, wt_ref, b_ref, kt_ref, gt_ref):
    dm = kt_ref.shape[1]
    full = lax.dot_general(wt_ref[...], h_ref[...], (((1,), (1,)), ((), ())),
                           preferred_element_type=F32) + b_ref[...]
    for c in range(kt_ref.shape[0]):
        kt_ref[c] = full[:dm, c * CHUNK:(c + 1) * CHUNK].astype(BF16)
    gt_ref[...] = full[dm:, :]


def _lane_scan(x, op, fill, reverse):
    n = x.shape[1]
    lane = lax.broadcasted_iota(jnp.int32, x.shape, 1)
    sh = 1
    while sh < n:
        if reverse:
            nb = jnp.where(lane < n - sh, pltpu.roll(x, n - sh, axis=1), fill)
        else:
            nb = jnp.where(lane >= sh, pltpu.roll(x, sh, axis=1), fill)
        x = op(x, nb)
        sh *= 2
    return x


def _mlstm_kernel(*refs, nc, n_sub, has_init, write_state, unroll):
    it = iter(refs)
    q_ref, kt_ref, v_ref, gr_ref = next(it), next(it), next(it), next(it)
    if has_init:
        c0_ref, n0_ref, m0_ref = next(it), next(it), next(it)
    h_ref = next(it)
    if write_state:
        cout_ref, nout_ref, mout_ref = next(it), next(it), next(it)
    c_sc, u_sc, wi_sc, cl_sc, wk_sc, vr_sc, dc_sc, mi_sc, mf_sc = (next(it) for _ in range(9))

    L = CHUNK
    dh = q_ref.shape[1]
    rows = u_sc.shape[1]
    blk, h_idx = pl.program_id(0), pl.program_id(1)
    row = lax.broadcasted_iota(jnp.int32, (L, L), 0)
    col = lax.broadcasted_iota(jnp.int32, (L, L), 1)
    visible = (col <= row, col >= row)
    no_inbound = (not has_init) and nc == 1
    ones_ext = jnp.ones((L, LANES), BF16)

    for d in range(N_DIR):
        rev = d == 1
        ig = gr_ref[0, d, 0, 0]
        lf = jax.nn.log_sigmoid(gr_ref[1, d, 0, 0])
        b = _lane_scan(lf, jnp.add, 0.0, rev)
        tot = jnp.broadcast_to(b[:, 0:1] if rev else b[:, L - 1:L], (rows, L))
        vrow = ig - b
        cm = _lane_scan(vrow, jnp.maximum, -jnp.inf, rev)
        g = tot - b + ig
        gmax = jnp.broadcast_to(jnp.max(g, axis=1, keepdims=True), (rows, L))
        mi_sc[d] = jnp.zeros((rows, L), F32)
        mf_sc[d] = jnp.zeros(mf_sc.shape[1:], F32)
        for jj in range(n_sub):
            if has_init:
                m = jnp.full((1, L), m0_ref[((blk * n_sub + jj) * N_DIR + d) * NH + h_idx], F32)
            else:
                m = jnp.zeros((1, L), F32)
            for c in (range(nc - 1, -1, -1) if rev else range(nc)):
                r = jj * nc + c
                mi_sc[d, r:r + 1, :] = m
                m = jnp.maximum(tot[r:r + 1] + m, gmax[r:r + 1])
            mf_sc[d, jj:jj + 1, :] = m
        m_in = mi_sc[d]
        m_out = jnp.maximum(tot + m_in, gmax)
        mm = jnp.maximum(m_in, cm)
        u_sc[d] = -mm
        wi_sc[d] = jnp.exp(m_in - mm)
        cl_sc[d] = jnp.exp(-(b + mm))
        wk_sc[d] = jnp.exp(g - m_out)
        vr_sc[d] = vrow
        dc_sc[d] = jnp.exp(tot + m_in - m_out)

    def chain(d, r, r0, s_raw):
        one = pl.ds(r, 1)
        qc = q_ref[pl.ds(r0, L), :]
        ktc = kt_ref[r]
        v_ext = jnp.concatenate([v_ref[pl.ds(r0, L), :], ones_ext], axis=1)
        if s_raw is None:
            s_raw = _bdot(qc, ktc)
        tile = jnp.concatenate([u_sc[d, one, :], wi_sc[d, one, :], cl_sc[d, one, :],
                                jnp.zeros((SUBLANES - 3, L), F32)], axis=0)
        cols = tile.T
        u_c, wi_c, cl_c = (cols[:, n:n + 1] for n in range(3))
        s = s_raw * jnp.exp(jnp.where(visible[d], u_c + vr_sc[d, one, :], -jnp.inf))
        acc = _bdot(s.astype(BF16), v_ext)
        if not no_inbound:
            acc = acc + wi_c * _bdot(qc, c_sc[d].astype(BF16))
        rden = 1.0 / jnp.maximum(jnp.abs(acc[:, dh:]), cl_c)
        h = acc[:, :dh] * jnp.concatenate([rden] * (dh // LANES), axis=1)
        kw = (ktc.astype(F32) * wk_sc[d, one, :]).astype(BF16)
        kv = _bdot(kw, v_ext)
        if no_inbound:
            c_sc[d] = kv
        else:
            c_sc[d] = dc_sc[d, one, :][:, 0:1] * c_sc[d] + kv
        return h

    def do_seq(j):
        base = j * (nc * L)
        for d in range(N_DIR):
            if has_init:
                n_rep = jnp.broadcast_to(n0_ref[j, 0, d, pl.ds(h_idx, 1), :], (LANES, dh)).T
                c_sc[d] = jnp.concatenate([c0_ref[j, 0, d, 0], n_rep], axis=1)
            elif not no_inbound:
                c_sc[d] = jnp.zeros(c_sc.shape[1:], F32)

        if nc == 1:
            r0 = pl.multiple_of(base, L)
            s_raw = _bdot(q_ref[pl.ds(r0, L), :], kt_ref[j])
            h_ref[pl.ds(r0, L), :] = chain(0, j, r0, s_raw) + chain(1, j, r0, s_raw)
        else:
            def step(i, first_visit):
                for d in range(N_DIR):
                    c = i if d == 0 else nc - 1 - i
                    r0 = pl.multiple_of(base + c * L, L)
                    h = chain(d, j * nc + c, r0, None)
                    if first_visit:
                        h_ref[pl.ds(r0, L), :] = h
                    else:
                        h_ref[pl.ds(r0, L), :] += h

            lax.fori_loop(0, nc // 2, lambda i, carry: (step(i, True), carry)[1], 0,
                          unroll=unroll)
            lax.fori_loop(nc // 2, nc, lambda i, carry: (step(i, False), carry)[1], 0,
                          unroll=unroll)

        if write_state:
            for d in range(N_DIR):
                cout_ref[j, 0, d, 0] = c_sc[d, :, :dh]
                nout_ref[j, d, 0] = c_sc[d, :, dh:].T[0:1, :]
                mout_ref[j, d, 0] = mf_sc[d, pl.ds(j, 1), :][:, :LANES]

    if n_sub == 1:
        do_seq(0)
    else:
        lax.fori_loop(0, n_sub, lambda j, carry: (do_seq(j), carry)[1], 0, unroll=unroll)


def _mlstm_call(z, kt, gates_rows, state, *, n_seq, t_len, n_sub, tok_block0, dh, unroll):
    nc = t_len // CHUNK
    rows = gates_rows.shape[-2]
    assert rows == n_sub * nc and rows % SUBLANES == 0 and (nc == 1 or nc % 2 == 0)
    has_init = state is not None
    write_state = not has_init
    kern = functools.partial(_mlstm_kernel, nc=nc, n_sub=n_sub, has_init=has_init,
                             write_state=write_state, unroll=unroll)
    tb = n_sub * t_len

    def z_spec(group):
        return pl.BlockSpec((tb, dh), lambda b, h: (tok_block0 + b, group * NH + h))

    in_specs = [z_spec(ZQ),
                pl.BlockSpec((rows, dh, CHUNK), lambda b, h: (tok_block0 + b, h, 0)),
                z_spec(ZV),
                pl.BlockSpec((2, N_DIR, 1, 1, rows, CHUNK), lambda b, h: (0, 0, h, b, 0, 0))]
    args = [z, kt, z, gates_rows]
    if has_init:
        c0, n0, m0 = state
        in_specs += [pl.BlockSpec((n_sub, 1, N_DIR, 1, dh, dh), lambda b, h: (b, 0, 0, h, 0, 0)),
                     pl.BlockSpec((n_sub, 1, N_DIR, NH, dh), lambda b, h: (b, 0, 0, 0, 0)),
                     pl.BlockSpec(memory_space=pltpu.SMEM)]
        args += [c0, n0, m0.reshape(-1)]
    out_specs = [pl.BlockSpec((tb, dh), lambda b, h: (b, h))]
    out_shape = [jax.ShapeDtypeStruct((n_seq * t_len, NH * dh), F32)]
    if write_state:
        out_specs += [pl.BlockSpec((n_sub, 1, N_DIR, 1, dh, dh), lambda b, h: (b, 0, 0, h, 0, 0)),
                      pl.BlockSpec((n_sub, N_DIR, 1, 1, dh), lambda b, h: (b, 0, h, 0, 0)),
                      pl.BlockSpec((n_sub, N_DIR, 1, 1, LANES), lambda b, h: (b, 0, h, 0, 0))]
        out_shape += [jax.ShapeDtypeStruct((n_seq, 1, N_DIR, NH, dh, dh), F32),
                      jax.ShapeDtypeStruct((n_seq, N_DIR, NH, 1, dh), F32),
                      jax.ShapeDtypeStruct((n_seq, N_DIR, NH, 1, LANES), F32)]
    per_row = pltpu.VMEM((N_DIR, rows, CHUNK), F32)
    return pl.pallas_call(
        kern,
        grid=(n_seq // n_sub, NH),
        in_specs=in_specs,
        out_specs=out_specs,
        out_shape=out_shape,
        scratch_shapes=[pltpu.VMEM((N_DIR, dh, dh + LANES), F32)] + [per_row] * 7
        + [pltpu.VMEM((N_DIR, max(SUBLANES, n_sub), CHUNK), F32)],
        compiler_params=_params(2),
        name="mlstm_lat" if has_init else "mlstm_ctx",
    )(*args)


def _mixout_kernel(hmc_ref, hml_ref, zo_ref, zb_ref, zc_ref, zx_ref, zgm_ref, zgs_ref, x1_ref,
                   mod_ref, mln_ref, cw_ref, cb_ref, wml_ref, wsc_ref, wo_ref, x2_ref, *,
                   n_ctx_blocks, ctx_seg, dh):
    is_ctx = pl.program_id(0) < n_ctx_blocks
    tm = x1_ref.shape[0]
    hm = jnp.where(is_ctx, hmc_ref[...], hml_ref[...])
    heads = []
    for hh in range(NH):
        x = hm[:, hh * dh:(hh + 1) * dh]
        heads.append(x * lax.rsqrt(jnp.mean(x * x, axis=-1, keepdims=True) + EPS))
    hn = jnp.concatenate(heads, axis=1) * mln_ref[...]
    ml = _bdot((zo_ref[...].astype(F32) * hn).astype(BF16), wml_ref[...])

    u = zc_ref[...].astype(F32) * zx_ref[...].astype(F32)
    t = lax.broadcasted_iota(jnp.int32, (tm, 1), 0)
    pos = jnp.where(is_ctx, t % ctx_seg, t % GRID_W)
    has_prev = pos != 0
    has_next = pos != jnp.where(is_ctx, ctx_seg - 1, GRID_W - 1)
    u_prev = jnp.where(has_prev, pltpu.roll(u, 1, axis=0), 0.0)
    u_next = jnp.where(has_next, pltpu.roll(u, tm - 1, axis=0), 0.0)
    uc = u_prev * cw_ref[0:1] + u * cw_ref[1:2] + u_next * cw_ref[2:3] + cb_ref[...]
    sc = _bdot((zb_ref[...].astype(F32) * uc).astype(BF16), wsc_ref[...])

    y = zgm_ref[...].astype(F32) * ml + zgs_ref[...].astype(F32) * sc
    x2_ref[...] = x1_ref[...] + mod_ref[0, 5:6] * _bdot(y.astype(BF16), wo_ref[...])


def kernel(x_prompt, x_sample, state_C, state_n, state_m, c, c_ctx, w_mod, b_mod, norm_g,
           ffn1_w1, ffn1_w2, w_in, b_in, conv_w, conv_b, ml_norm, w_ml_out, w_sc_out, w_o,
           ffn2_w1, ffn2_w2, final_norm):
    assert w_mod.shape[0] == 1, "single trunk layer only"
    bp, tp, dm = x_prompt.shape
    bs, ts, _ = x_sample.shape
    n_ctx, n_lat = bp * tp, bs * ts
    n_tok = n_ctx + n_lat
    d_ff = ffn1_w2.shape[1]
    ml_dim = w_ml_out.shape[1]
    sc_dim = w_sc_out.shape[1]
    dh = ml_dim // NH
    n_gate = 2 * N_DIR * NH
    assert ml_dim == dm and sc_dim == dm and dh % 256 == 0 and dm == TN_IN
    assert w_in.shape[2] == 9 * dm + n_gate and n_gate < LANES
    assert n_ctx % TM_FFN == 0 and ts % TM_FFN == 0 and n_tok % TM_IN == 0
    assert tp % TM_MIX == 0 and TM_MIX % GRID_W == 0 and TM_IN % CHUNK == 0
    assert tp == CHUNK and ts % (2 * CHUNK) == 0 and bp % CTX_SEQS == 0 and n_ctx % ts == 0
    n_cond = 1 + bs

    cond = jnp.concatenate([c_ctx[None, :], c], axis=0)
    condT = jnp.pad(cond.T, ((0, 0), (0, SUBLANES - n_cond)))
    mod = _mod_call(condT, w_mod[0], b_mod[0][None, :], n_cond).reshape(SUBLANES, N_MOD, dm)

    def mod_spec(tm):
        ncb, per_seq = n_ctx // tm, ts // tm
        return pl.BlockSpec((1, N_MOD, dm),
                            lambda i, *_: (jnp.where(i < ncb, 0, 1 + (i - ncb) // per_seq), 0, 0))

    w_in0, b_in0 = w_in[0], b_in[0]
    gate_lo = 4 * ml_dim
    ncb = n_ctx // TM_FFN
    nb = n_tok // TM_FFN
    chunks = _ff_chunks(d_ff)
    tok_spec = pl.BlockSpec((TM_FFN, dm), lambda i: (i, 0))
    x1, h2 = pl.pallas_call(
        functools.partial(_ffn1_kernel, chunks=chunks, n_ctx_blocks=ncb),
        grid=(nb,),
        in_specs=[pl.BlockSpec((TM_FFN, dm), lambda i: (jnp.minimum(i, ncb - 1), 0)),
                  pl.BlockSpec((TM_FFN, dm), lambda i: (jnp.maximum(i - ncb, 0), 0)),
                  mod_spec(TM_FFN), _resident((3, dm)),
                  _resident((dm, 2 * d_ff)), _resident((d_ff, dm))],
        out_specs=[tok_spec, tok_spec],
        out_shape=[jax.ShapeDtypeStruct((n_tok, dm), F32), jax.ShapeDtypeStruct((n_tok, dm), BF16)],
        compiler_params=_params(1),
        name="ffn1",
    )(x_prompt.reshape(n_ctx, dm), x_sample.reshape(n_lat, dm), mod, norm_g[0],
      ffn1_w1[0].astype(BF16), ffn1_w2[0].astype(BF16))

    n_lead = gate_lo // TN_IN - 1
    b_main = jnp.concatenate([b_in0[:ml_dim], b_in0[2 * ml_dim:gate_lo],
                              b_in0[gate_lo + n_gate:]])[None, :]
    n_main = b_main.shape[1]

    def w_block(j):
        return jnp.where(j == 0, 0, j + 1)

    z = pl.pallas_call(
        functools.partial(_inproj_kernel, n_lead=n_lead, shift=n_gate),
        grid=(n_main // TN_IN, n_tok // TM_IN),
        in_specs=[pl.BlockSpec((TM_IN, dm), lambda j, i: (i, 0)),
                  pl.BlockSpec((dm, TN_IN), lambda j, i: (0, w_block(j))),
                  pl.BlockSpec((dm, TN_IN), lambda j, i: (0, w_block(j) + 1)),
                  pl.BlockSpec((1, TN_IN), lambda j, i: (0, j))],
        out_specs=pl.BlockSpec((TM_IN, TN_IN), lambda j, i: (i, j)),
        out_shape=jax.ShapeDtypeStruct((n_tok, n_main), BF16),
        scratch_shapes=[pltpu.VMEM((dm, TN_IN), BF16)],
        compiler_params=_params(2),
        name="inproj",
    )(h2, w_in0, w_in0, b_main)

    kscale = dh ** -0.5
    w_kg = jnp.concatenate([w_in0[:, ml_dim:2 * ml_dim] * kscale,
                            w_in0[:, gate_lo:gate_lo + n_gate]], axis=1).T.astype(BF16)
    b_kg = jnp.concatenate([b_in0[ml_dim:2 * ml_dim] * kscale,
                            b_in0[gate_lo:gate_lo + n_gate]])[:, None]
    kt, gates_t = pl.pallas_call(
        _kproj_kernel,
        grid=(n_tok // TM_IN,),
        in_specs=[pl.BlockSpec((TM_IN, dm), lambda i: (i, 0)), _resident((dm + n_gate, dm)),
                  _resident((dm + n_gate, 1))],
        out_specs=[pl.BlockSpec((TM_IN // CHUNK, dm, CHUNK), lambda i: (i, 0, 0)),
                   pl.BlockSpec((n_gate, TM_IN), lambda i: (0, i))],
        out_shape=[jax.ShapeDtypeStruct((n_tok // CHUNK, dm, CHUNK), BF16),
                   jax.ShapeDtypeStruct((n_gate, n_tok), F32)],
        compiler_params=_params(1),
        name="kproj",
    )(h2, w_kg, b_kg)

    g_rows = gates_t.reshape(2, N_DIR, NH, n_tok)
    rows_ctx = CTX_SEQS * (tp // CHUNK)
    hm_ctx, new_c, new_n, new_m = _mlstm_call(
        z, kt, g_rows[..., :n_ctx].reshape(2, N_DIR, NH, bp // CTX_SEQS, rows_ctx, CHUNK), None,
        n_seq=bp, t_len=tp, n_sub=CTX_SEQS, tok_block0=0, dh=dh, unroll=2)
    (hm_lat,) = _mlstm_call(
        z, kt, g_rows[..., n_ctx:].reshape(2, N_DIR, NH, bs, ts // CHUNK, CHUNK),
        (state_C, state_n, state_m),
        n_seq=bs, t_len=ts, n_sub=1, tok_block0=n_ctx // ts, dh=dh, unroll=2)

    ncb_m = n_ctx // TM_MIX
    tokm = pl.BlockSpec((TM_MIX, dm), lambda i: (i, 0))

    def zcol(j):
        return pl.BlockSpec((TM_MIX, dm), lambda i: (i, j))

    x2 = pl.pallas_call(
        functools.partial(_mixout_kernel, n_ctx_blocks=ncb_m, ctx_seg=tp, dh=dh),
        grid=(n_tok // TM_MIX,),
        in_specs=[pl.BlockSpec((TM_MIX, dm), lambda i: (jnp.minimum(i, ncb_m - 1), 0)),
                  pl.BlockSpec((TM_MIX, dm), lambda i: (jnp.maximum(i - ncb_m, 0), 0)),
                  zcol(ZO), zcol(ZB), zcol(ZC), zcol(ZX), zcol(ZGM), zcol(ZGS), tokm,
                  mod_spec(TM_MIX), _resident((1, dm)), _resident((3, dm)), _resident((1, dm)),
                  _resident((dm, dm)), _resident((dm, dm)), _resident((dm, dm))],
        out_specs=tokm,
        out_shape=jax.ShapeDtypeStruct((n_tok, dm), F32),
        compiler_params=_params(1),
        name="mixout",
    )(hm_ctx, hm_lat, z, z, z, z, z, z, x1, mod, ml_norm[0][None, :], conv_w[0],
      conv_b[0][None, :], w_ml_out[0].astype(BF16), w_sc_out[0].astype(BF16), w_o[0].astype(BF16))

    y_p, y_s = pl.pallas_call(
        functools.partial(_ffn2_kernel, chunks=chunks, n_ctx_blocks=ncb),
        grid=(nb,),
        in_specs=[tok_spec, mod_spec(TM_FFN), _resident((3, dm)), _resident((1, dm)),
                  _resident((dm, 2 * d_ff)), _resident((d_ff, dm))],
        out_specs=[pl.BlockSpec((TM_FFN, dm), lambda i: (jnp.minimum(i, ncb - 1), 0)),
                   pl.BlockSpec((TM_FFN, dm), lambda i: (jnp.maximum(i - ncb, 0), 0))],
        out_shape=[jax.ShapeDtypeStruct((n_ctx, dm), F32), jax.ShapeDtypeStruct((n_lat, dm), F32)],
        compiler_params=_params(1),
        name="ffn2",
    )(x2, mod, norm_g[0], final_norm[None, :], ffn2_w1[0].astype(BF16), ffn2_w2[0].astype(BF16))

    return (y_p.reshape(bp, tp, dm), y_s.reshape(bs, ts, dm), new_c,
            new_n.reshape(bp, 1, N_DIR, NH, dh), new_m[..., 0, 0].reshape(bp, 1, N_DIR, NH))
```

```python
import functools

import jax
import jax.numpy as jnp
from jax import lax
from jax.experimental import pallas as pl
from jax.experimental.pallas import tpu as pltpu

F32 = jnp.float32
BF16 = jnp.bfloat16

NH = 4
N_DIR = 2
N_MOD = 9
GRID_W = 64
EPS = 1e-6
LANES = 128
SUBLANES = 8
VMEM_LIMIT = 56 * 1024 * 1024

TM_FFN = 512
TM_MIX = 256
TM_IN = 2048
TN_IN = 1024
CHUNK = 256
CTX_SEQS = 8
ZQ, ZV, ZO, ZB, ZC, ZX, ZGM, ZGS = range(8)
SIG_GROUPS = (ZO, ZGM, ZGS)


def _params(n_axes):
    return pltpu.CompilerParams(dimension_semantics=("arbitrary",) * n_axes,
                                vmem_limit_bytes=VMEM_LIMIT)


def _resident(shape):
    zeros = (0,) * len(shape)
    return pl.BlockSpec(shape, lambda *_: zeros, pipeline_mode=pl.Buffered(1))


def _rms(x, g):
    return x * lax.rsqrt(jnp.mean(x * x, axis=-1, keepdims=True) + EPS) * g


def _bdot(a, b):
    return jnp.dot(a, b, preferred_element_type=F32)


def _mod_kernel(condT_ref, w_ref, b_ref, o_ref, *, n_cond):
    s = jax.nn.silu(condT_ref[...])
    w = w_ref[...]
    rows = [jnp.sum(w * s[:, r:r + 1], axis=0, keepdims=True) for r in range(n_cond)]
    rows.append(jnp.zeros((SUBLANES - n_cond, w.shape[1]), F32))
    o_ref[...] = jnp.concatenate(rows, axis=0) + b_ref[...]


def _mod_call(condT, w_mod, b_mod, n_cond):
    d, n = w_mod.shape
    tn = 1024
    return pl.pallas_call(
        functools.partial(_mod_kernel, n_cond=n_cond),
        grid=(n // tn,),
        in_specs=[pl.BlockSpec((d, SUBLANES), lambda j: (0, 0)),
                  pl.BlockSpec((d, tn), lambda j: (0, j)),
                  pl.BlockSpec((1, tn), lambda j: (0, j))],
        out_specs=pl.BlockSpec((SUBLANES, tn), lambda j: (0, j)),
        out_shape=jax.ShapeDtypeStruct((SUBLANES, n), F32),
        compiler_params=_params(1),
        name="mod",
    )(condT, w_mod, b_mod)


def _swiglu_residual(x, mod, g_norm, w1_ref, w2_ref, chunks, row0):
    d_ff = w2_ref.shape[0]
    h = (_rms(x, g_norm) * (1.0 + mod[row0 + 1:row0 + 2]) + mod[row0:row0 + 1]).astype(BF16)
    acc = None
    for lo, hi in chunks:
        a = _bdot(h, w1_ref[:, lo:hi])
        b = _bdot(h, w1_ref[:, d_ff + lo:d_ff + hi])
        part = _bdot((jax.nn.silu(a) * b).astype(BF16), w2_ref[lo:hi, :])
        acc = part if acc is None else acc + part
    return x + (0.5 * mod[row0 + 2:row0 + 3]) * acc


def _ffn1_kernel(xp_ref, xs_ref, mod_ref, g_ref, w1_ref, w2_ref, x1_ref, h2_ref, *,
                 chunks, n_ctx_blocks):
    is_ctx = pl.program_id(0) < n_ctx_blocks
    x = jnp.where(is_ctx, xp_ref[...], xs_ref[...])
    mod = mod_ref[0]
    x1 = _swiglu_residual(x, mod, g_ref[0:1], w1_ref, w2_ref, chunks, 0)
    x1_ref[...] = x1
    h2_ref[...] = (_rms(x1, g_ref[1:2]) * (1.0 + mod[4:5]) + mod[3:4]).astype(BF16)


def _ffn2_kernel(x_ref, mod_ref, g_ref, fin_ref, w1_ref, w2_ref, yp_ref, ys_ref, *,
                 chunks, n_ctx_blocks):
    i = pl.program_id(0)
    x3 = _swiglu_residual(x_ref[...], mod_ref[0], g_ref[2:3], w1_ref, w2_ref, chunks, 6)
    y = _rms(x3, fin_ref[...])

    @pl.when(i < n_ctx_blocks)
    def _():
        yp_ref[...] = y

    @pl.when(i >= n_ctx_blocks)
    def _():
        ys_ref[...] = y


def _ff_chunks(d_ff):
    tile = 256
    half = max(tile, (d_ff // 2 + tile - 1) // tile * tile)
    return ((0, half), (half, d_ff)) if half < d_ff else ((0, d_ff),)


def _inproj_kernel(h_ref, wa_ref, wb_ref, b_ref, z_ref, w_sc, *, n_lead, shift):
    j, i = pl.program_id(0), pl.program_id(1)

    @pl.when(jnp.logical_and(i == 0, j < n_lead))
    def _():
        w_sc[...] = wa_ref[...].astype(BF16)

    @pl.when(jnp.logical_and(i == 0, j >= n_lead))
    def _():
        w_sc[...] = jnp.concatenate([wa_ref[shift:, :], wb_ref[:shift, :]], axis=0).astype(BF16)

    def project():
        return lax.dot_general(h_ref[...], w_sc[...], (((1,), (1,)), ((), ())),
                               preferred_element_type=F32) + b_ref[...]

    is_sig = functools.reduce(jnp.logical_or, [j == g for g in SIG_GROUPS])

    @pl.when(is_sig)
    def _():
        z_ref[...] = (0.5 * jnp.tanh(0.5 * project()) + 0.5).astype(BF16)

    @pl.when(jnp.logical_not(is_sig))
    def _():
        z_ref[...] = project().astype(BF16)


def _kproj_kernel(h_ref, wt_ref, b_ref, kt_ref, gt_ref):
    dm = kt_ref.shape[1]
    full = lax.dot_general(wt_ref[...], h_ref[...], (((1,), (1,)), ((), ())),
                           preferred_element_type=F32) + b_ref[...]
    for c in range(kt_ref.shape[0]):
        kt_ref[c] = full[:dm, c * CHUNK:(c + 1) * CHUNK].astype(BF16)
    gt_ref[...] = full[dm:, :]


def _lane_scan(x, op, fill, reverse):
    n = x.shape[1]
    lane = lax.broadcasted_iota(jnp.int32, x.shape, 1)
    sh = 1
    while sh < n:
        if reverse:
            nb = jnp.where(lane < n - sh, pltpu.roll(x, n - sh, axis=1), fill)
        else:
            nb = jnp.where(lane >= sh, pltpu.roll(x, sh, axis=1), fill)
        x = op(x, nb)
        sh *= 2
    return x


def _mlstm_kernel(*refs, nc, n_sub, has_init, write_state, unroll):
    it = iter(refs)
    q_ref, kt_ref, v_ref, gr_ref, mln_ref = (next(it) for _ in range(5))
    if has_init:
        c0_ref, n0_ref, m0_ref = next(it), next(it), next(it)
    hn_ref = next(it)
    if write_state:
        cout_ref, nout_ref, mout_ref = next(it), next(it), next(it)
    c_sc, u_sc, wi_sc, cl_sc, wk_sc, vr_sc, dc_sc, mi_sc, mf_sc = (next(it) for _ in range(9))
    if nc > 1:
        h_sc = next(it)

    def head_norm(h):
        mean_sq = _bdot((h * h).astype(BF16), jnp.full((h.shape[1],) * 2, 1.0 / h.shape[1], BF16))
        return (h * lax.rsqrt(mean_sq + EPS) * mln_ref[...]).astype(BF16)

    L = CHUNK
    dh = q_ref.shape[1]
    rows = u_sc.shape[1]
    blk, h_idx = pl.program_id(0), pl.program_id(1)
    row = lax.broadcasted_iota(jnp.int32, (L, L), 0)
    col = lax.broadcasted_iota(jnp.int32, (L, L), 1)
    visible = (col <= row, col >= row)
    no_inbound = (not has_init) and nc == 1
    ones_ext = jnp.ones((L, LANES), BF16)

    for d in range(N_DIR):
        rev = d == 1
        ig = gr_ref[0, d, 0, 0]
        lf = jax.nn.log_sigmoid(gr_ref[1, d, 0, 0])
        b = _lane_scan(lf, jnp.add, 0.0, rev)
        tot = jnp.broadcast_to(b[:, 0:1] if rev else b[:, L - 1:L], (rows, L))
        vrow = ig - b
        cm = _lane_scan(vrow, jnp.maximum, -jnp.inf, rev)
        g = tot - b + ig
        gmax = jnp.broadcast_to(jnp.max(g, axis=1, keepdims=True), (rows, L))
        mi_sc[d] = jnp.zeros((rows, L), F32)
        mf_sc[d] = jnp.zeros(mf_sc.shape[1:], F32)
        for jj in range(n_sub):
            if has_init:
                m = jnp.full((1, L), m0_ref[((blk * n_sub + jj) * N_DIR + d) * NH + h_idx], F32)
            else:
                m = jnp.zeros((1, L), F32)
            for c in (range(nc - 1, -1, -1) if rev else range(nc)):
                r = jj * nc + c
                mi_sc[d, r:r + 1, :] = m
                m = jnp.maximum(tot[r:r + 1] + m, gmax[r:r + 1])
            mf_sc[d, jj:jj + 1, :] = m
        m_in = mi_sc[d]
        m_out = jnp.maximum(tot + m_in, gmax)
        mm = jnp.maximum(m_in, cm)
        u_sc[d] = -mm
        wi_sc[d] = jnp.exp(m_in - mm)
        cl_sc[d] = jnp.exp(-(b + mm))
        wk_sc[d] = jnp.exp(g - m_out)
        vr_sc[d] = vrow
        dc_sc[d] = jnp.exp(tot + m_in - m_out)

    def chain(d, r, r0, s_raw):
        one = pl.ds(r, 1)
        qc = q_ref[pl.ds(r0, L), :]
        ktc = kt_ref[r]
        v_ext = jnp.concatenate([v_ref[pl.ds(r0, L), :], ones_ext], axis=1)
        if s_raw is None:
            s_raw = _bdot(qc, ktc)
        tile = jnp.concatenate([u_sc[d, one, :], wi_sc[d, one, :], cl_sc[d, one, :],
                                jnp.zeros((SUBLANES - 3, L), F32)], axis=0)
        cols = tile.T
        u_c, wi_c, cl_c = (cols[:, n:n + 1] for n in range(3))
        s = s_raw * jnp.exp(jnp.where(visible[d], u_c + vr_sc[d, one, :], -jnp.inf))
        acc = _bdot(s.astype(BF16), v_ext)
        if not no_inbound:
            acc = acc + wi_c * _bdot(qc, c_sc[d].astype(BF16))
        rden = 1.0 / jnp.maximum(jnp.abs(acc[:, dh:]), cl_c)
        h = acc[:, :dh] * jnp.concatenate([rden] * (dh // LANES), axis=1)
        kw = (ktc.astype(F32) * wk_sc[d, one, :]).astype(BF16)
        kv = _bdot(kw, v_ext)
        if no_inbound:
            c_sc[d] = kv
        else:
            c_sc[d] = dc_sc[d, one, :][:, 0:1] * c_sc[d] + kv
        return h

    def do_seq(j):
        base = j * (nc * L)
        for d in range(N_DIR):
            if has_init:
                n_rep = jnp.broadcast_to(n0_ref[j, 0, d, pl.ds(h_idx, 1), :], (LANES, dh)).T
                c_sc[d] = jnp.concatenate([c0_ref[j, 0, d, 0], n_rep], axis=1)
            elif not no_inbound:
                c_sc[d] = jnp.zeros(c_sc.shape[1:], F32)

        if nc == 1:
            r0 = pl.multiple_of(base, L)
            s_raw = _bdot(q_ref[pl.ds(r0, L), :], kt_ref[j])
            hn_ref[pl.ds(r0, L), :] = head_norm(chain(0, j, r0, s_raw) + chain(1, j, r0, s_raw))
        else:
            def step(i, first_visit):
                for d in range(N_DIR):
                    c = i if d == 0 else nc - 1 - i
                    r0 = pl.multiple_of(base + c * L, L)
                    h = chain(d, j * nc + c, r0, None)
                    if first_visit:
                        h_sc[pl.ds(r0, L), :] = h
                    else:
                        hn_ref[pl.ds(r0, L), :] = head_norm(h_sc[pl.ds(r0, L), :] + h)

            lax.fori_loop(0, nc // 2, lambda i, carry: (step(i, True), carry)[1], 0,
                          unroll=unroll)
            lax.fori_loop(nc // 2, nc, lambda i, carry: (step(i, False), carry)[1], 0,
                          unroll=unroll)

        if write_state:
            for d in range(N_DIR):
                cout_ref[j, 0, d, 0] = c_sc[d, :, :dh]
                nout_ref[j, d, 0] = c_sc[d, :, dh:].T[0:1, :]
                mout_ref[j, d, 0] = mf_sc[d, pl.ds(j, 1), :][:, :LANES]

    if n_sub == 1:
        do_seq(0)
    else:
        lax.fori_loop(0, n_sub, lambda j, carry: (do_seq(j), carry)[1], 0, unroll=unroll)


def _mlstm_call(z, kt, gates_rows, ml_norm, state, *, n_seq, t_len, n_sub, tok_block0, dh, unroll):
    nc = t_len // CHUNK
    rows = gates_rows.shape[-2]
    assert rows == n_sub * nc and rows % SUBLANES == 0 and (nc == 1 or nc % 2 == 0)
    has_init = state is not None
    write_state = not has_init
    kern = functools.partial(_mlstm_kernel, nc=nc, n_sub=n_sub, has_init=has_init,
                             write_state=write_state, unroll=unroll)
    tb = n_sub * t_len

    def z_spec(group):
        return pl.BlockSpec((tb, dh), lambda b, h: (tok_block0 + b, group * NH + h))

    in_specs = [z_spec(ZQ),
                pl.BlockSpec((rows, dh, CHUNK), lambda b, h: (tok_block0 + b, h, 0)),
                z_spec(ZV),
                pl.BlockSpec((2, N_DIR, 1, 1, rows, CHUNK), lambda b, h: (0, 0, h, b, 0, 0)),
                pl.BlockSpec((1, dh), lambda b, h: (0, h))]
    args = [z, kt, z, gates_rows, ml_norm]
    if has_init:
        c0, n0, m0 = state
        in_specs += [pl.BlockSpec((n_sub, 1, N_DIR, 1, dh, dh), lambda b, h: (b, 0, 0, h, 0, 0)),
                     pl.BlockSpec((n_sub, 1, N_DIR, NH, dh), lambda b, h: (b, 0, 0, 0, 0)),
                     pl.BlockSpec(memory_space=pltpu.SMEM)]
        args += [c0, n0, m0.reshape(-1)]
    out_specs = [pl.BlockSpec((tb, dh), lambda b, h: (b, h))]
    out_shape = [jax.ShapeDtypeStruct((n_seq * t_len, NH * dh), BF16)]
    if write_state:
        out_specs += [pl.BlockSpec((n_sub, 1, N_DIR, 1, dh, dh), lambda b, h: (b, 0, 0, h, 0, 0)),
                      pl.BlockSpec((n_sub, N_DIR, 1, 1, dh), lambda b, h: (b, 0, h, 0, 0)),
                      pl.BlockSpec((n_sub, N_DIR, 1, 1, LANES), lambda b, h: (b, 0, h, 0, 0))]
        out_shape += [jax.ShapeDtypeStruct((n_seq, 1, N_DIR, NH, dh, dh), F32),
                      jax.ShapeDtypeStruct((n_seq, N_DIR, NH, 1, dh), F32),
                      jax.ShapeDtypeStruct((n_seq, N_DIR, NH, 1, LANES), F32)]
    per_row = pltpu.VMEM((N_DIR, rows, CHUNK), F32)
    return pl.pallas_call(
        kern,
        grid=(n_seq // n_sub, NH),
        in_specs=in_specs,
        out_specs=out_specs,
        out_shape=out_shape,
        scratch_shapes=[pltpu.VMEM((N_DIR, dh, dh + LANES), F32)] + [per_row] * 7
        + [pltpu.VMEM((N_DIR, max(SUBLANES, n_sub), CHUNK), F32)]
        + ([pltpu.VMEM((tb, dh), F32)] if nc > 1 else []),
        compiler_params=_params(2),
        name="mlstm_lat" if has_init else "mlstm_ctx",
    )(*args)


def _mixout_kernel(hnc_ref, hnl_ref, zo_ref, zb_ref, zc_ref, zx_ref, zgm_ref, zgs_ref, x1_ref,
                   mod_ref, cw_ref, cb_ref, wml_ref, wsc_ref, wo_ref, x2_ref, *,
                   n_ctx_blocks, ctx_seg):
    is_ctx = pl.program_id(0) < n_ctx_blocks
    tm = x1_ref.shape[0]
    hn = jnp.where(is_ctx, hnc_ref[...], hnl_ref[...])
    ml = _bdot(zo_ref[...] * hn, wml_ref[...])

    u = zc_ref[...].astype(F32) * zx_ref[...].astype(F32)
    t = lax.broadcasted_iota(jnp.int32, (tm, 1), 0)
    pos = jnp.where(is_ctx, t % ctx_seg, t % GRID_W)
    has_prev = pos != 0
    has_next = pos != jnp.where(is_ctx, ctx_seg - 1, GRID_W - 1)
    u_prev = jnp.where(has_prev, pltpu.roll(u, 1, axis=0), 0.0)
    u_next = jnp.where(has_next, pltpu.roll(u, tm - 1, axis=0), 0.0)
    uc = u_prev * cw_ref[0:1] + u * cw_ref[1:2] + u_next * cw_ref[2:3] + cb_ref[...]
    sc = _bdot(zb_ref[...] * uc.astype(BF16), wsc_ref[...])

    y = zgm_ref[...].astype(F32) * ml + zgs_ref[...].astype(F32) * sc
    x2_ref[...] = x1_ref[...] + mod_ref[0, 5:6] * _bdot(y.astype(BF16), wo_ref[...])


def kernel(x_prompt, x_sample, state_C, state_n, state_m, c, c_ctx, w_mod, b_mod, norm_g,
           ffn1_w1, ffn1_w2, w_in, b_in, conv_w, conv_b, ml_norm, w_ml_out, w_sc_out, w_o,
           ffn2_w1, ffn2_w2, final_norm):
    assert w_mod.shape[0] == 1, "single trunk layer only"
    bp, tp, dm = x_prompt.shape
    bs, ts, _ = x_sample.shape
    n_ctx, n_lat = bp * tp, bs * ts
    n_tok = n_ctx + n_lat
    d_ff = ffn1_w2.shape[1]
    ml_dim = w_ml_out.shape[1]
    sc_dim = w_sc_out.shape[1]
    dh = ml_dim // NH
    n_gate = 2 * N_DIR * NH
    assert ml_dim == dm and sc_dim == dm and dh % 256 == 0 and dm == TN_IN
    assert w_in.shape[2] == 9 * dm + n_gate and n_gate < LANES
    assert n_ctx % TM_FFN == 0 and ts % TM_FFN == 0 and n_tok % TM_IN == 0
    assert tp % TM_MIX == 0 and TM_MIX % GRID_W == 0 and TM_IN % CHUNK == 0
    assert tp == CHUNK and ts % (2 * CHUNK) == 0 and bp % CTX_SEQS == 0 and n_ctx % ts == 0
    n_cond = 1 + bs

    cond = jnp.concatenate([c_ctx[None, :], c], axis=0)
    condT = jnp.pad(cond.T, ((0, 0), (0, SUBLANES - n_cond)))
    mod = _mod_call(condT, w_mod[0], b_mod[0][None, :], n_cond).reshape(SUBLANES, N_MOD, dm)

    def mod_spec(tm):
        ncb, per_seq = n_ctx // tm, ts // tm
        return pl.BlockSpec((1, N_MOD, dm),
                            lambda i, *_: (jnp.where(i < ncb, 0, 1 + (i - ncb) // per_seq), 0, 0))

    w_in0, b_in0 = w_in[0], b_in[0]
    gate_lo = 4 * ml_dim
    ncb = n_ctx // TM_FFN
    nb = n_tok // TM_FFN
    chunks = _ff_chunks(d_ff)
    tok_spec = pl.BlockSpec((TM_FFN, dm), lambda i: (i, 0))
    x1, h2 = pl.pallas_call(
        functools.partial(_ffn1_kernel, chunks=chunks, n_ctx_blocks=ncb),
        grid=(nb,),
        in_specs=[pl.BlockSpec((TM_FFN, dm), lambda i: (jnp.minimum(i, ncb - 1), 0)),
                  pl.BlockSpec((TM_FFN, dm), lambda i: (jnp.maximum(i - ncb, 0), 0)),
                  mod_spec(TM_FFN), _resident((3, dm)),
                  _resident((dm, 2 * d_ff)), _resident((d_ff, dm))],
        out_specs=[tok_spec, tok_spec],
        out_shape=[jax.ShapeDtypeStruct((n_tok, dm), F32), jax.ShapeDtypeStruct((n_tok, dm), BF16)],
        compiler_params=_params(1),
        name="ffn1",
    )(x_prompt.reshape(n_ctx, dm), x_sample.reshape(n_lat, dm), mod, norm_g[0],
      ffn1_w1[0].astype(BF16), ffn1_w2[0].astype(BF16))

    w_in_t = w_in0.T
    n_lead = gate_lo // TN_IN - 1
    b_main = jnp.concatenate([b_in0[:ml_dim], b_in0[2 * ml_dim:gate_lo],
                              b_in0[gate_lo + n_gate:]])[None, :]
    n_main = b_main.shape[1]

    def w_block(j):
        return jnp.where(j == 0, 0, j + 1)

    z = pl.pallas_call(
        functools.partial(_inproj_kernel, n_lead=n_lead, shift=n_gate),
        grid=(n_main // TN_IN, n_tok // TM_IN),
        in_specs=[pl.BlockSpec((TM_IN, dm), lambda j, i: (i, 0)),
                  pl.BlockSpec((TN_IN, dm), lambda j, i: (w_block(j), 0)),
                  pl.BlockSpec((TN_IN, dm), lambda j, i: (w_block(j) + 1, 0)),
                  pl.BlockSpec((1, TN_IN), lambda j, i: (0, j))],
        out_specs=pl.BlockSpec((TM_IN, TN_IN), lambda j, i: (i, j)),
        out_shape=jax.ShapeDtypeStruct((n_tok, n_main), BF16),
        scratch_shapes=[pltpu.VMEM((TN_IN, dm), BF16)],
        compiler_params=_params(2),
        name="inproj",
    )(h2, w_in_t, w_in_t, b_main)

    kscale = dh ** -0.5
    w_kg = jnp.concatenate([w_in_t[ml_dim:2 * ml_dim] * kscale,
                            w_in_t[gate_lo:gate_lo + n_gate]], axis=0).astype(BF16)
    b_kg = jnp.concatenate([b_in0[ml_dim:2 * ml_dim] * kscale,
                            b_in0[gate_lo:gate_lo + n_gate]])[:, None]
    kt, gates_t = pl.pallas_call(
        _kproj_kernel,
        grid=(n_tok // TM_IN,),
        in_specs=[pl.BlockSpec((TM_IN, dm), lambda i: (i, 0)), _resident((dm + n_gate, dm)),
                  _resident((dm + n_gate, 1))],
        out_specs=[pl.BlockSpec((TM_IN // CHUNK, dm, CHUNK), lambda i: (i, 0, 0)),
                   pl.BlockSpec((n_gate, TM_IN), lambda i: (0, i))],
        out_shape=[jax.ShapeDtypeStruct((n_tok // CHUNK, dm, CHUNK), BF16),
                   jax.ShapeDtypeStruct((n_gate, n_tok), F32)],
        compiler_params=_params(1),
        name="kproj",
    )(h2, w_kg, b_kg)

    g_rows = gates_t.reshape(2, N_DIR, NH, n_tok)
    rows_ctx = CTX_SEQS * (tp // CHUNK)
    hn_ctx, new_c, new_n, new_m = _mlstm_call(
        z, kt, g_rows[..., :n_ctx].reshape(2, N_DIR, NH, bp // CTX_SEQS, rows_ctx, CHUNK),
        ml_norm, None,
        n_seq=bp, t_len=tp, n_sub=CTX_SEQS, tok_block0=0, dh=dh, unroll=2)
    (hn_lat,) = _mlstm_call(
        z, kt, g_rows[..., n_ctx:].reshape(2, N_DIR, NH, bs, ts // CHUNK, CHUNK),
        ml_norm, (state_C, state_n, state_m),
        n_seq=bs, t_len=ts, n_sub=1, tok_block0=n_ctx // ts, dh=dh, unroll=2)

    ncb_m = n_ctx // TM_MIX
    tokm = pl.BlockSpec((TM_MIX, dm), lambda i: (i, 0))

    def zcol(j):
        return pl.BlockSpec((TM_MIX, dm), lambda i: (i, j))

    x2 = pl.pallas_call(
        functools.partial(_mixout_kernel, n_ctx_blocks=ncb_m, ctx_seg=tp),
        grid=(n_tok // TM_MIX,),
        in_specs=[pl.BlockSpec((TM_MIX, dm), lambda i: (jnp.minimum(i, ncb_m - 1), 0)),
                  pl.BlockSpec((TM_MIX, dm), lambda i: (jnp.maximum(i - ncb_m, 0), 0)),
                  zcol(ZO), zcol(ZB), zcol(ZC), zcol(ZX), zcol(ZGM), zcol(ZGS), tokm,
                  mod_spec(TM_MIX), _resident((3, dm)), _resident((1, dm)),
                  _resident((dm, dm)), _resident((dm, dm)), _resident((dm, dm))],
        out_specs=tokm,
        out_shape=jax.ShapeDtypeStruct((n_tok, dm), F32),
        compiler_params=_params(1),
        name="mixout",
    )(hn_ctx, hn_lat, z, z, z, z, z, z, x1, mod, conv_w[0],
      conv_b[0][None, :], w_ml_out[0].astype(BF16), w_sc_out[0].astype(BF16), w_o[0].astype(BF16))

    y_p, y_s = pl.pallas_call(
        functools.partial(_ffn2_kernel, chunks=chunks, n_ctx_blocks=ncb),
        grid=(nb,),
        in_specs=[tok_spec, mod_spec(TM_FFN), _resident((3, dm)), _resident((1, dm)),
                  _resident((dm, 2 * d_ff)), _resident((d_ff, dm))],
        out_specs=[pl.BlockSpec((TM_FFN, dm), lambda i: (jnp.minimum(i, ncb - 1), 0)),
                   pl.BlockSpec((TM_FFN, dm), lambda i: (jnp.maximum(i - ncb, 0), 0))],
        out_shape=[jax.ShapeDtypeStruct((n_ctx, dm), F32), jax.ShapeDtypeStruct((n_lat, dm), F32)],
        compiler_params=_params(1),
        name="ffn2",
    )(x2, mod, norm_g[0], final_norm[None, :], ffn2_w1[0].astype(BF16), ffn2_w2[0].astype(BF16))

    return (y_p.reshape(bp, tp, dm), y_s.reshape(bs, ts, dm), new_c,
            new_n.reshape(bp, 1, N_DIR, NH, dh), new_m[..., 0, 0].reshape(bp, 1, N_DIR, NH))
```

```python
import functools

import jax
import jax.numpy as jnp
from jax import lax
from jax.experimental import pallas as pl
from jax.experimental.pallas import tpu as pltpu

F32 = jnp.float32
BF16 = jnp.bfloat16

NH = 4
N_DIR = 2
N_MOD = 9
GRID_W = 64
EPS = 1e-6
LANES = 128
SUBLANES = 8
VMEM_LIMIT = 56 * 1024 * 1024

TM_FFN = 512
FFN_SUB = 256
TM_MIX = 512
MIX_SUB = 256
TM_IN = 2048
TN_IN = 1024
TM_K = 1024
MOD_TN = 2304
CHUNK = 256
CTX_SEQS = 8
ZQ, ZV, ZO, ZB, ZC, ZX, ZGM, ZGS = range(8)
SIG_GROUPS = (ZO, ZGM, ZGS)


def _params(n_axes):
    return pltpu.CompilerParams(dimension_semantics=("arbitrary",) * n_axes,
                                vmem_limit_bytes=VMEM_LIMIT)


def _resident(shape):
    zeros = (0,) * len(shape)
    return pl.BlockSpec(shape, lambda *_: zeros, pipeline_mode=pl.Buffered(1))


def _rms(x, g):
    return x * lax.rsqrt(jnp.mean(x * x, axis=-1, keepdims=True) + EPS) * g


def _bdot(a, b):
    return jnp.dot(a, b, preferred_element_type=F32)


def _mod_kernel(condT_ref, w_ref, b_ref, o_ref, *, n_cond):
    s = jax.nn.silu(condT_ref[...])
    w = w_ref[...]
    rows = [jnp.sum(w * s[:, r:r + 1], axis=0, keepdims=True) for r in range(n_cond)]
    rows.append(jnp.zeros((SUBLANES - n_cond, w.shape[1]), F32))
    o_ref[...] = jnp.concatenate(rows, axis=0) + b_ref[...]


def _mod_call(condT, w_mod, b_mod, n_cond):
    d, n = w_mod.shape
    tn = MOD_TN
    assert n % tn == 0
    return pl.pallas_call(
        functools.partial(_mod_kernel, n_cond=n_cond),
        grid=(n // tn,),
        in_specs=[pl.BlockSpec((d, SUBLANES), lambda j: (0, 0)),
                  pl.BlockSpec((d, tn), lambda j: (0, j)),
                  pl.BlockSpec((1, tn), lambda j: (0, j))],
        out_specs=pl.BlockSpec((SUBLANES, tn), lambda j: (0, j)),
        out_shape=jax.ShapeDtypeStruct((SUBLANES, n), F32),
        compiler_params=_params(1),
        name="mod",
    )(condT, w_mod, b_mod)


def _swiglu_residual(x, mod, g_norm, w1_ref, w2_ref, chunks, row0):
    d_ff = w2_ref.shape[0]
    h = (_rms(x, g_norm) * (1.0 + mod[row0 + 1:row0 + 2]) + mod[row0:row0 + 1]).astype(BF16)
    acc = None
    for lo, hi in chunks:
        a = _bdot(h, w1_ref[:, lo:hi])
        b = _bdot(h, w1_ref[:, d_ff + lo:d_ff + hi])
        part = _bdot((jax.nn.silu(a) * b).astype(BF16), w2_ref[lo:hi, :])
        acc = part if acc is None else acc + part
    return x + (0.5 * mod[row0 + 2:row0 + 3]) * acc


def _row_parts(tm):
    return [pl.ds(r, FFN_SUB) for r in range(0, tm, FFN_SUB)]


def _ffn1_kernel(xp_ref, xs_ref, mod_ref, g_ref, w1_ref, w2_ref, x1_ref, h2_ref, *,
                 chunks, n_ctx_blocks):
    is_ctx = pl.program_id(0) < n_ctx_blocks
    mod = mod_ref[0]
    for rows in _row_parts(x1_ref.shape[0]):
        x = jnp.where(is_ctx, xp_ref[rows, :], xs_ref[rows, :])
        x1 = _swiglu_residual(x, mod, g_ref[0:1], w1_ref, w2_ref, chunks, 0)
        x1_ref[rows, :] = x1
        h2_ref[rows, :] = (_rms(x1, g_ref[1:2]) * (1.0 + mod[4:5]) + mod[3:4]).astype(BF16)


def _ffn2_kernel(x_ref, mod_ref, g_ref, fin_ref, w1_ref, w2_ref, yp_ref, ys_ref, *,
                 chunks, n_ctx_blocks):
    i = pl.program_id(0)
    mod = mod_ref[0]
    y = jnp.concatenate(
        [_rms(_swiglu_residual(x_ref[rows, :], mod, g_ref[2:3], w1_ref, w2_ref, chunks, 6),
              fin_ref[...]) for rows in _row_parts(x_ref.shape[0])], axis=0)

    @pl.when(i < n_ctx_blocks)
    def _():
        yp_ref[...] = y

    @pl.when(i >= n_ctx_blocks)
    def _():
        ys_ref[...] = y


def _ff_chunks(d_ff):
    tile = 256
    half = max(tile, (d_ff // 2 + tile - 1) // tile * tile)
    return ((0, half), (half, d_ff)) if half < d_ff else ((0, d_ff),)


def _inproj_kernel(h_ref, wa_ref, wb_ref, b_ref, z_ref, w_sc, *, n_lead, shift):
    j, i = pl.program_id(0), pl.program_id(1)

    @pl.when(jnp.logical_and(i == 0, j < n_lead))
    def _():
        w_sc[...] = wa_ref[...].astype(BF16)

    @pl.when(jnp.logical_and(i == 0, j >= n_lead))
    def _():
        w_sc[...] = jnp.concatenate([wa_ref[shift:, :], wb_ref[:shift, :]], axis=0).astype(BF16)

    def project():
        return lax.dot_general(h_ref[...], w_sc[...], (((1,), (1,)), ((), ())),
                               preferred_element_type=F32) + b_ref[...]

    is_sig = functools.reduce(jnp.logical_or, [j == g for g in SIG_GROUPS])

    @pl.when(is_sig)
    def _():
        z_ref[...] = (0.5 * jnp.tanh(0.5 * project()) + 0.5).astype(BF16)

    @pl.when(jnp.logical_not(is_sig))
    def _():
        z_ref[...] = project().astype(BF16)


def _kproj_kernel(h_ref, wt_ref, b_ref, kt_ref, gt_ref):
    dm = kt_ref.shape[1]
    full = lax.dot_general(wt_ref[...], h_ref[...], (((1,), (1,)), ((), ())),
                           preferred_element_type=F32) + b_ref[...]
    for c in range(kt_ref.shape[0]):
        kt_ref[c] = full[:dm, c * CHUNK:(c + 1) * CHUNK].astype(BF16)
    gt_ref[...] = full[dm:, :]


def _lane_scan(x, op, fill, reverse):
    n = x.shape[1]
    lane = lax.broadcasted_iota(jnp.int32, x.shape, 1)
    sh = 1
    while sh < n:
        if reverse:
            nb = jnp.where(lane < n - sh, pltpu.roll(x, n - sh, axis=1), fill)
        else:
            nb = jnp.where(lane >= sh, pltpu.roll(x, sh, axis=1), fill)
        x = op(x, nb)
        sh *= 2
    return x


def _mlstm_kernel(*refs, nc, n_sub, has_init, write_state, unroll):
    it = iter(refs)
    q_ref, kt_ref, v_ref, gr_ref, mln_ref = (next(it) for _ in range(5))
    if has_init:
        c0_ref, n0_ref, m0_ref = next(it), next(it), next(it)
    hn_ref = next(it)
    if write_state:
        cout_ref, nout_ref, mout_ref = next(it), next(it), next(it)
    c_sc, u_sc, wi_sc, cl_sc, wk_sc, vr_sc, dc_sc, mi_sc, mf_sc = (next(it) for _ in range(9))
    if nc > 1:
        h_sc = next(it)

    def head_norm(h):
        mean_sq = _bdot((h * h).astype(BF16), jnp.full((h.shape[1],) * 2, 1.0 / h.shape[1], BF16))
        return (h * lax.rsqrt(mean_sq + EPS) * mln_ref[...]).astype(BF16)

    L = CHUNK
    dh = q_ref.shape[1]
    rows = u_sc.shape[1]
    blk, h_idx = pl.program_id(0), pl.program_id(1)
    row = lax.broadcasted_iota(jnp.int32, (L, L), 0)
    col = lax.broadcasted_iota(jnp.int32, (L, L), 1)
    visible = (col <= row, col >= row)
    no_inbound = (not has_init) and nc == 1
    ones_ext = jnp.ones((L, LANES), BF16)

    for d in range(N_DIR):
        rev = d == 1
        ig = gr_ref[0, d, 0, 0]
        lf = jax.nn.log_sigmoid(gr_ref[1, d, 0, 0])
        b = _lane_scan(lf, jnp.add, 0.0, rev)
        tot = jnp.broadcast_to(b[:, 0:1] if rev else b[:, L - 1:L], (rows, L))
        vrow = ig - b
        cm = _lane_scan(vrow, jnp.maximum, -jnp.inf, rev)
        g = tot - b + ig
        gmax = jnp.broadcast_to(jnp.max(g, axis=1, keepdims=True), (rows, L))
        mi_sc[d] = jnp.zeros((rows, L), F32)
        mf_sc[d] = jnp.zeros(mf_sc.shape[1:], F32)
        for jj in range(n_sub):
            if has_init:
                m = jnp.full((1, L), m0_ref[((blk * n_sub + jj) * N_DIR + d) * NH + h_idx], F32)
            else:
                m = jnp.zeros((1, L), F32)
            for c in (range(nc - 1, -1, -1) if rev else range(nc)):
                r = jj * nc + c
                mi_sc[d, r:r + 1, :] = m
                m = jnp.maximum(tot[r:r + 1] + m, gmax[r:r + 1])
            mf_sc[d, jj:jj + 1, :] = m
        m_in = mi_sc[d]
        m_out = jnp.maximum(tot + m_in, gmax)
        mm = jnp.maximum(m_in, cm)
        u_sc[d] = -mm
        wi_sc[d] = jnp.exp(m_in - mm)
        cl_sc[d] = jnp.exp(-(b + mm))
        wk_sc[d] = jnp.exp(g - m_out)
        vr_sc[d] = vrow
        dc_sc[d] = jnp.exp(tot + m_in - m_out)

    def chain(d, r, r0, s_raw):
        one = pl.ds(r, 1)
        qc = q_ref[pl.ds(r0, L), :]
        ktc = kt_ref[r]
        v_ext = jnp.concatenate([v_ref[pl.ds(r0, L), :], ones_ext], axis=1)
        if s_raw is None:
            s_raw = _bdot(qc, ktc)
        tile = jnp.concatenate([u_sc[d, one, :], wi_sc[d, one, :], cl_sc[d, one, :],
                                jnp.zeros((SUBLANES - 3, L), F32)], axis=0)
        cols = tile.T
        u_c, wi_c, cl_c = (cols[:, n:n + 1] for n in range(3))
        s = s_raw * jnp.exp(jnp.where(visible[d], u_c + vr_sc[d, one, :], -jnp.inf))
        acc = _bdot(s.astype(BF16), v_ext)
        if not no_inbound:
            acc = acc + wi_c * _bdot(qc, c_sc[d].astype(BF16))
        rden = 1.0 / jnp.maximum(jnp.abs(acc[:, dh:]), cl_c)
        h = acc[:, :dh] * jnp.concatenate([rden] * (dh // LANES), axis=1)
        kw = (ktc.astype(F32) * wk_sc[d, one, :]).astype(BF16)
        kv = _bdot(kw, v_ext)
        if no_inbound:
            c_sc[d] = kv
        else:
            c_sc[d] = dc_sc[d, one, :][:, 0:1] * c_sc[d] + kv
        return h

    def do_seq(j):
        base = j * (nc * L)
        for d in range(N_DIR):
            if has_init:
                n_rep = jnp.broadcast_to(n0_ref[j, 0, d, pl.ds(h_idx, 1), :], (LANES, dh)).T
                c_sc[d] = jnp.concatenate([c0_ref[j, 0, d, 0], n_rep], axis=1)
            elif not no_inbound:
                c_sc[d] = jnp.zeros(c_sc.shape[1:], F32)

        if nc == 1:
            r0 = pl.multiple_of(base, L)
            s_raw = _bdot(q_ref[pl.ds(r0, L), :], kt_ref[j])
            hn_ref[pl.ds(r0, L), :] = head_norm(chain(0, j, r0, s_raw) + chain(1, j, r0, s_raw))
        else:
            def step(i, first_visit):
                for d in range(N_DIR):
                    c = i if d == 0 else nc - 1 - i
                    r0 = pl.multiple_of(base + c * L, L)
                    h = chain(d, j * nc + c, r0, None)
                    if first_visit:
                        h_sc[pl.ds(r0, L), :] = h
                    else:
                        hn_ref[pl.ds(r0, L), :] = head_norm(h_sc[pl.ds(r0, L), :] + h)

            lax.fori_loop(0, nc // 2, lambda i, carry: (step(i, True), carry)[1], 0,
                          unroll=unroll)
            lax.fori_loop(nc // 2, nc, lambda i, carry: (step(i, False), carry)[1], 0,
                          unroll=unroll)

        if write_state:
            for d in range(N_DIR):
                cout_ref[j, 0, d, 0] = c_sc[d, :, :dh]
                nout_ref[j, d, 0] = c_sc[d, :, dh:].T[0:1, :]
                mout_ref[j, d, 0] = mf_sc[d, pl.ds(j, 1), :][:, :LANES]

    if n_sub == 1:
        do_seq(0)
    else:
        lax.fori_loop(0, n_sub, lambda j, carry: (do_seq(j), carry)[1], 0, unroll=unroll)


def _mlstm_call(z, kt, gates_rows, ml_norm, state, *, n_seq, t_len, n_sub, tok_block0, dh, unroll):
    nc = t_len // CHUNK
    rows = gates_rows.shape[-2]
    assert rows == n_sub * nc and rows % SUBLANES == 0 and (nc == 1 or nc % 2 == 0)
    has_init = state is not None
    write_state = not has_init
    kern = functools.partial(_mlstm_kernel, nc=nc, n_sub=n_sub, has_init=has_init,
                             write_state=write_state, unroll=unroll)
    tb = n_sub * t_len

    def z_spec(group):
        return pl.BlockSpec((tb, dh), lambda b, h: (tok_block0 + b, group * NH + h))

    in_specs = [z_spec(ZQ),
                pl.BlockSpec((rows, dh, CHUNK), lambda b, h: (tok_block0 + b, h, 0)),
                z_spec(ZV),
                pl.BlockSpec((2, N_DIR, 1, 1, rows, CHUNK), lambda b, h: (0, 0, h, b, 0, 0)),
                pl.BlockSpec((1, dh), lambda b, h: (0, h))]
    args = [z, kt, z, gates_rows, ml_norm]
    if has_init:
        c0, n0, m0 = state
        in_specs += [pl.BlockSpec((n_sub, 1, N_DIR, 1, dh, dh), lambda b, h: (b, 0, 0, h, 0, 0)),
                     pl.BlockSpec((n_sub, 1, N_DIR, NH, dh), lambda b, h: (b, 0, 0, 0, 0)),
                     pl.BlockSpec(memory_space=pltpu.SMEM)]
        args += [c0, n0, m0.reshape(-1)]
    out_specs = [pl.BlockSpec((tb, dh), lambda b, h: (b, h))]
    out_shape = [jax.ShapeDtypeStruct((n_seq * t_len, NH * dh), BF16)]
    if write_state:
        out_specs += [pl.BlockSpec((n_sub, 1, N_DIR, 1, dh, dh), lambda b, h: (b, 0, 0, h, 0, 0)),
                      pl.BlockSpec((n_sub, N_DIR, 1, 1, dh), lambda b, h: (b, 0, h, 0, 0)),
                      pl.BlockSpec((n_sub, N_DIR, 1, 1, LANES), lambda b, h: (b, 0, h, 0, 0))]
        out_shape += [jax.ShapeDtypeStruct((n_seq, 1, N_DIR, NH, dh, dh), F32),
                      jax.ShapeDtypeStruct((n_seq, N_DIR, NH, 1, dh), F32),
                      jax.ShapeDtypeStruct((n_seq, N_DIR, NH, 1, LANES), F32)]
    per_row = pltpu.VMEM((N_DIR, rows, CHUNK), F32)
    return pl.pallas_call(
        kern,
        grid=(n_seq // n_sub, NH),
        in_specs=in_specs,
        out_specs=out_specs,
        out_shape=out_shape,
        scratch_shapes=[pltpu.VMEM((N_DIR, dh, dh + LANES), F32)] + [per_row] * 7
        + [pltpu.VMEM((N_DIR, max(SUBLANES, n_sub), CHUNK), F32)]
        + ([pltpu.VMEM((tb, dh), F32)] if nc > 1 else []),
        compiler_params=_params(2),
        name="mlstm_lat" if has_init else "mlstm_ctx",
    )(*args)


def _mixout_kernel(hnc_ref, hnl_ref, zo_ref, zb_ref, zc_ref, zx_ref, zgm_ref, zgs_ref, x1_ref,
                   mod_ref, cw_ref, cb_ref, wml_ref, wsc_ref, wo_ref, x2_ref, *,
                   n_ctx_blocks, ctx_seg):
    is_ctx = pl.program_id(0) < n_ctx_blocks
    sub = MIX_SUB
    t = lax.broadcasted_iota(jnp.int32, (sub, 1), 0)
    pos = jnp.where(is_ctx, t % ctx_seg, t % GRID_W)
    has_prev = pos != 0
    has_next = pos != jnp.where(is_ctx, ctx_seg - 1, GRID_W - 1)
    for r in range(0, x1_ref.shape[0], sub):
        rows = pl.ds(r, sub)
        hn = jnp.where(is_ctx, hnc_ref[rows, :], hnl_ref[rows, :])
        ml = _bdot(zo_ref[rows, :] * hn, wml_ref[...])

        u = zc_ref[rows, :].astype(F32) * zx_ref[rows, :].astype(F32)
        u_prev = jnp.where(has_prev, pltpu.roll(u, 1, axis=0), 0.0)
        u_next = jnp.where(has_next, pltpu.roll(u, sub - 1, axis=0), 0.0)
        uc = u_prev * cw_ref[0:1] + u * cw_ref[1:2] + u_next * cw_ref[2:3] + cb_ref[...]
        sc = _bdot(zb_ref[rows, :] * uc.astype(BF16), wsc_ref[...])

        y = zgm_ref[rows, :].astype(F32) * ml + zgs_ref[rows, :].astype(F32) * sc
        x2_ref[rows, :] = x1_ref[rows, :] + mod_ref[0, 5:6] * _bdot(y.astype(BF16), wo_ref[...])


def kernel(x_prompt, x_sample, state_C, state_n, state_m, c, c_ctx, w_mod, b_mod, norm_g,
           ffn1_w1, ffn1_w2, w_in, b_in, conv_w, conv_b, ml_norm, w_ml_out, w_sc_out, w_o,
           ffn2_w1, ffn2_w2, final_norm):
    assert w_mod.shape[0] == 1, "single trunk layer only"
    bp, tp, dm = x_prompt.shape
    bs, ts, _ = x_sample.shape
    n_ctx, n_lat = bp * tp, bs * ts
    n_tok = n_ctx + n_lat
    d_ff = ffn1_w2.shape[1]
    ml_dim = w_ml_out.shape[1]
    sc_dim = w_sc_out.shape[1]
    dh = ml_dim // NH
    n_gate = 2 * N_DIR * NH
    assert ml_dim == dm and sc_dim == dm and dh % 256 == 0 and dm == TN_IN
    assert w_in.shape[2] == 9 * dm + n_gate and n_gate < LANES
    assert n_ctx % TM_FFN == 0 and ts % TM_FFN == 0 and n_tok % TM_IN == 0
    assert n_ctx % TM_MIX == 0 and ts % TM_MIX == 0 and MIX_SUB % tp == 0 and MIX_SUB % GRID_W == 0
    assert TM_K % CHUNK == 0 and n_tok % TM_K == 0
    assert tp == CHUNK and ts % (2 * CHUNK) == 0 and bp % CTX_SEQS == 0 and n_ctx % ts == 0
    n_cond = 1 + bs

    cond = jnp.concatenate([c_ctx[None, :], c], axis=0)
    condT = jnp.pad(cond.T, ((0, 0), (0, SUBLANES - n_cond)))
    mod = _mod_call(condT, w_mod[0], b_mod[0][None, :], n_cond).reshape(SUBLANES, N_MOD, dm)

    def mod_spec(tm):
        ncb, per_seq = n_ctx // tm, ts // tm
        return pl.BlockSpec((1, N_MOD, dm),
                            lambda i, *_: (jnp.where(i < ncb, 0, 1 + (i - ncb) // per_seq), 0, 0))

    w_in0, b_in0 = w_in[0], b_in[0]
    gate_lo = 4 * ml_dim
    ncb = n_ctx // TM_FFN
    nb = n_tok // TM_FFN
    chunks = _ff_chunks(d_ff)
    tok_spec = pl.BlockSpec((TM_FFN, dm), lambda i: (i, 0))
    x1, h2 = pl.pallas_call(
        functools.partial(_ffn1_kernel, chunks=chunks, n_ctx_blocks=ncb),
        grid=(nb,),
        in_specs=[pl.BlockSpec((TM_FFN, dm), lambda i: (jnp.minimum(i, ncb - 1), 0)),
                  pl.BlockSpec((TM_FFN, dm), lambda i: (jnp.maximum(i - ncb, 0), 0)),
                  mod_spec(TM_FFN), _resident((3, dm)),
                  _resident((dm, 2 * d_ff)), _resident((d_ff, dm))],
        out_specs=[tok_spec, tok_spec],
        out_shape=[jax.ShapeDtypeStruct((n_tok, dm), F32), jax.ShapeDtypeStruct((n_tok, dm), BF16)],
        compiler_params=_params(1),
        name="ffn1",
    )(x_prompt.reshape(n_ctx, dm), x_sample.reshape(n_lat, dm), mod, norm_g[0],
      ffn1_w1[0].astype(BF16), ffn1_w2[0].astype(BF16))

    w_in_t = w_in0.T
    n_lead = gate_lo // TN_IN - 1
    b_main = jnp.concatenate([b_in0[:ml_dim], b_in0[2 * ml_dim:gate_lo],
                              b_in0[gate_lo + n_gate:]])[None, :]
    n_main = b_main.shape[1]

    def w_block(j):
        return jnp.where(j == 0, 0, j + 1)

    z = pl.pallas_call(
        functools.partial(_inproj_kernel, n_lead=n_lead, shift=n_gate),
        grid=(n_main // TN_IN, n_tok // TM_IN),
        in_specs=[pl.BlockSpec((TM_IN, dm), lambda j, i: (i, 0)),
                  pl.BlockSpec((TN_IN, dm), lambda j, i: (w_block(j), 0)),
                  pl.BlockSpec((TN_IN, dm), lambda j, i: (w_block(j) + 1, 0)),
                  pl.BlockSpec((1, TN_IN), lambda j, i: (0, j))],
        out_specs=pl.BlockSpec((TM_IN, TN_IN), lambda j, i: (i, j)),
        out_shape=jax.ShapeDtypeStruct((n_tok, n_main), BF16),
        scratch_shapes=[pltpu.VMEM((TN_IN, dm), BF16)],
        compiler_params=_params(2),
        name="inproj",
    )(h2, w_in_t, w_in_t, b_main)

    kscale = dh ** -0.5
    w_kg = jnp.concatenate([w_in_t[ml_dim:2 * ml_dim] * kscale,
                            w_in_t[gate_lo:gate_lo + n_gate]], axis=0).astype(BF16)
    b_kg = jnp.concatenate([b_in0[ml_dim:2 * ml_dim] * kscale,
                            b_in0[gate_lo:gate_lo + n_gate]])[:, None]
    kt, gates_t = pl.pallas_call(
        _kproj_kernel,
        grid=(n_tok // TM_K,),
        in_specs=[pl.BlockSpec((TM_K, dm), lambda i: (i, 0)), _resident((dm + n_gate, dm)),
                  _resident((dm + n_gate, 1))],
        out_specs=[pl.BlockSpec((TM_K // CHUNK, dm, CHUNK), lambda i: (i, 0, 0)),
                   pl.BlockSpec((n_gate, TM_K), lambda i: (0, i))],
        out_shape=[jax.ShapeDtypeStruct((n_tok // CHUNK, dm, CHUNK), BF16),
                   jax.ShapeDtypeStruct((n_gate, n_tok), F32)],
        compiler_params=_params(1),
        name="kproj",
    )(h2, w_kg, b_kg)

    g_rows = gates_t.reshape(2, N_DIR, NH, n_tok)
    rows_ctx = CTX_SEQS * (tp // CHUNK)
    hn_ctx, new_c, new_n, new_m = _mlstm_call(
        z, kt, g_rows[..., :n_ctx].reshape(2, N_DIR, NH, bp // CTX_SEQS, rows_ctx, CHUNK),
        ml_norm, None,
        n_seq=bp, t_len=tp, n_sub=CTX_SEQS, tok_block0=0, dh=dh, unroll=4)
    (hn_lat,) = _mlstm_call(
        z, kt, g_rows[..., n_ctx:].reshape(2, N_DIR, NH, bs, ts // CHUNK, CHUNK),
        ml_norm, (state_C, state_n, state_m),
        n_seq=bs, t_len=ts, n_sub=1, tok_block0=n_ctx // ts, dh=dh, unroll=4)

    ncb_m = n_ctx // TM_MIX
    tokm = pl.BlockSpec((TM_MIX, dm), lambda i: (i, 0))

    def zcol(j):
        return pl.BlockSpec((TM_MIX, dm), lambda i: (i, j))

    x2 = pl.pallas_call(
        functools.partial(_mixout_kernel, n_ctx_blocks=ncb_m, ctx_seg=tp),
        grid=(n_tok // TM_MIX,),
        in_specs=[pl.BlockSpec((TM_MIX, dm), lambda i: (jnp.minimum(i, ncb_m - 1), 0)),
                  pl.BlockSpec((TM_MIX, dm), lambda i: (jnp.maximum(i - ncb_m, 0), 0)),
                  zcol(ZO), zcol(ZB), zcol(ZC), zcol(ZX), zcol(ZGM), zcol(ZGS), tokm,
                  mod_spec(TM_MIX), _resident((3, dm)), _resident((1, dm)),
                  _resident((dm, dm)), _resident((dm, dm)), _resident((dm, dm))],
        out_specs=tokm,
        out_shape=jax.ShapeDtypeStruct((n_tok, dm), F32),
        compiler_params=_params(1),
        name="mixout",
    )(hn_ctx, hn_lat, z, z, z, z, z, z, x1, mod, conv_w[0],
      conv_b[0][None, :], w_ml_out[0].astype(BF16), w_sc_out[0].astype(BF16), w_o[0].astype(BF16))

    y_p, y_s = pl.pallas_call(
        functools.partial(_ffn2_kernel, chunks=chunks, n_ctx_blocks=ncb),
        grid=(nb,),
        in_specs=[tok_spec, mod_spec(TM_FFN), _resident((3, dm)), _resident((1, dm)),
                  _resident((dm, 2 * d_ff)), _resident((d_ff, dm))],
        out_specs=[pl.BlockSpec((TM_FFN, dm), lambda i: (jnp.minimum(i, ncb - 1), 0)),
                   pl.BlockSpec((TM_FFN, dm), lambda i: (jnp.maximum(i - ncb, 0), 0))],
        out_shape=[jax.ShapeDtypeStruct((n_ctx, dm), F32), jax.ShapeDtypeStruct((n_lat, dm), F32)],
        compiler_params=_params(1),
        name="ffn2",
    )(x2, mod, norm_g[0], final_norm[None, :], ffn2_w1[0].astype(BF16), ffn2_w2[0].astype(BF16))

    return (y_p.reshape(bp, tp, dm), y_s.reshape(bs, ts, dm), new_c,
            new_n.reshape(bp, 1, N_DIR, NH, dh), new_m[..., 0, 0].reshape(bp, 1, N_DIR, NH))
```

```python
import functools

import jax
import jax.numpy as jnp
from jax import lax
from jax.experimental import pallas as pl
from jax.experimental.pallas import tpu as pltpu

F32 = jnp.float32
BF16 = jnp.bfloat16

NH = 4
N_DIR = 2
N_MOD = 9
GRID_W = 64
EPS = 1e-6
LANES = 128
SUBLANES = 8
VMEM_LIMIT = 56 * 1024 * 1024

TM_FFN = 512
FFN_SUB = 256
FFN_WSTEPS = 11
TM_MIX = 512
MIX_SUB = 256
TM_IN = 2048
TN_IN = 1024
TM_K = 1024
CHUNK = 256
CTX_SEQS = 8
ZQ, ZV, ZO, ZB, ZC, ZX, ZGM, ZGS = range(8)
SIG_GROUPS = (ZO, ZGM, ZGS)


def _params(n_axes):
    return pltpu.CompilerParams(dimension_semantics=("arbitrary",) * n_axes,
                                vmem_limit_bytes=VMEM_LIMIT)


def _resident(shape):
    zeros = (0,) * len(shape)
    return pl.BlockSpec(shape, lambda *_: zeros, pipeline_mode=pl.Buffered(1))


def _rms(x, g):
    return x * lax.rsqrt(jnp.mean(x * x, axis=-1, keepdims=True) + EPS) * g


def _bdot(a, b):
    return jnp.dot(a, b, preferred_element_type=F32)


def _mod_rows(mod_ref, t, n_ctx_blocks, blocks_per_seq):
    r = jnp.where(t < n_ctx_blocks, 0, 1 + (t - n_ctx_blocks) // blocks_per_seq)
    return [mod_ref[k, pl.ds(r, 1), :] for k in range(N_MOD)]


def _swiglu_residual(x, mod, g_norm, w1_ref, w2_ref, chunks, row0):
    d_ff = w2_ref.shape[0]
    h = (_rms(x, g_norm) * (1.0 + mod[row0 + 1]) + mod[row0]).astype(BF16)
    acc = None
    for lo, hi in chunks:
        a = _bdot(h, w1_ref[:, lo:hi])
        b = _bdot(h, w1_ref[:, d_ff + lo:d_ff + hi])
        part = _bdot((jax.nn.silu(a) * b).astype(BF16), w2_ref[lo:hi, :])
        acc = part if acc is None else acc + part
    return x + (0.5 * mod[row0 + 2]) * acc


def _row_parts(tm):
    return [pl.ds(r, FFN_SUB) for r in range(0, tm, FFN_SUB)]


def _stage_weights(i, w1f_ref, w2f_ref, w1_sc, w2_sc, also=None):
    c1, c2 = w1f_ref.shape[1], w2f_ref.shape[0]
    for c in range(FFN_WSTEPS):
        @pl.when(i == c)
        def _(c=c):
            w1_sc[:, c * c1:(c + 1) * c1] = w1f_ref[...].astype(BF16)
            w2_sc[c * c2:(c + 1) * c2, :] = w2f_ref[...].astype(BF16)
            if also is not None:
                also(c)


def _ffn1_kernel(xp_ref, xs_ref, condT_ref, wmod_ref, bmod_ref, g_ref, w1f_ref, w2f_ref,
                 x1_ref, h2_ref, mod_ref, w1_sc, w2_sc, *,
                 chunks, n_ctx_blocks, blocks_per_seq, n_cond):
    i = pl.program_id(0)

    def mod_chunk(c):
        if c < N_MOD:
            s = jax.nn.silu(condT_ref[...])
            w = wmod_ref[...]
            rows = [jnp.sum(w * s[:, r:r + 1], axis=0, keepdims=True) for r in range(n_cond)]
            rows.append(jnp.zeros((SUBLANES - n_cond, w.shape[1]), F32))
            mod_ref[c] = jnp.concatenate(rows, axis=0) + bmod_ref[c:c + 1, :]

    _stage_weights(i, w1f_ref, w2f_ref, w1_sc, w2_sc, also=mod_chunk)

    @pl.when(i >= FFN_WSTEPS)
    def _():
        t = i - FFN_WSTEPS
        is_ctx = t < n_ctx_blocks
        mod = _mod_rows(mod_ref, t, n_ctx_blocks, blocks_per_seq)
        for rows in _row_parts(x1_ref.shape[0]):
            x = jnp.where(is_ctx, xp_ref[rows, :], xs_ref[rows, :])
            x1 = _swiglu_residual(x, mod, g_ref[0:1], w1_sc, w2_sc, chunks, 0)
            x1_ref[rows, :] = x1
            h2_ref[rows, :] = (_rms(x1, g_ref[1:2]) * (1.0 + mod[4]) + mod[3]).astype(BF16)


def _ffn2_kernel(x_ref, mod_ref, g_ref, fin_ref, w1f_ref, w2f_ref, yp_ref, ys_ref, w1_sc, w2_sc, *,
                 chunks, n_ctx_blocks, blocks_per_seq):
    i = pl.program_id(0)
    _stage_weights(i, w1f_ref, w2f_ref, w1_sc, w2_sc)

    @pl.when(i >= FFN_WSTEPS)
    def _():
        mod = _mod_rows(mod_ref, i - FFN_WSTEPS, n_ctx_blocks, blocks_per_seq)
        y = jnp.concatenate(
            [_rms(_swiglu_residual(x_ref[rows, :], mod, g_ref[2:3], w1_sc, w2_sc, chunks, 6),
                  fin_ref[...]) for rows in _row_parts(x_ref.shape[0])], axis=0)

        @pl.when(i - FFN_WSTEPS < n_ctx_blocks)
        def _():
            yp_ref[...] = y

        @pl.when(i - FFN_WSTEPS >= n_ctx_blocks)
        def _():
            ys_ref[...] = y


def _ff_chunks(d_ff):
    tile = 256
    half = max(tile, (d_ff // 2 + tile - 1) // tile * tile)
    return ((0, half), (half, d_ff)) if half < d_ff else ((0, d_ff),)


def _inproj_kernel(h_ref, wa_ref, wb_ref, b_ref, z_ref, w_sc, *, n_lead, shift):
    j, i = pl.program_id(0), pl.program_id(1)

    @pl.when(jnp.logical_and(i == 0, j < n_lead))
    def _():
        w_sc[...] = wa_ref[...].astype(BF16)

    @pl.when(jnp.logical_and(i == 0, j >= n_lead))
    def _():
        w_sc[...] = jnp.concatenate([wa_ref[shift:, :], wb_ref[:shift, :]], axis=0).astype(BF16)

    def project():
        return lax.dot_general(h_ref[...], w_sc[...], (((1,), (1,)), ((), ())),
                               preferred_element_type=F32) + b_ref[...]

    is_sig = functools.reduce(jnp.logical_or, [j == g for g in SIG_GROUPS])

    @pl.when(is_sig)
    def _():
        z_ref[...] = (0.5 * jnp.tanh(0.5 * project()) + 0.5).astype(BF16)

    @pl.when(jnp.logical_not(is_sig))
    def _():
        z_ref[...] = project().astype(BF16)


def _kproj_kernel(h_ref, wt_ref, b_ref, kt_ref, gt_ref):
    dm = kt_ref.shape[1]
    full = lax.dot_general(wt_ref[...], h_ref[...], (((1,), (1,)), ((), ())),
                           preferred_element_type=F32) + b_ref[...]
    for c in range(kt_ref.shape[0]):
        kt_ref[c] = full[:dm, c * CHUNK:(c + 1) * CHUNK].astype(BF16)
    gt_ref[...] = full[dm:, :]


def _lane_scan(x, op, fill, reverse):
    n = x.shape[1]
    lane = lax.broadcasted_iota(jnp.int32, x.shape, 1)
    sh = 1
    while sh < n:
        if reverse:
            nb = jnp.where(lane < n - sh, pltpu.roll(x, n - sh, axis=1), fill)
        else:
            nb = jnp.where(lane >= sh, pltpu.roll(x, sh, axis=1), fill)
        x = op(x, nb)
        sh *= 2
    return x


def _mlstm_kernel(*refs, nc, n_sub, has_init, write_state, unroll):
    it = iter(refs)
    q_ref, kt_ref, v_ref, gr_ref, mln_ref = (next(it) for _ in range(5))
    if has_init:
        c0_ref, n0_ref, m0_ref = next(it), next(it), next(it)
    hn_ref = next(it)
    if write_state:
        cout_ref, nout_ref, mout_ref = next(it), next(it), next(it)
    c_sc, u_sc, wi_sc, cl_sc, wk_sc, vr_sc, dc_sc, mi_sc, mf_sc = (next(it) for _ in range(9))
    if nc > 1:
        h_sc = next(it)

    def head_norm(h):
        mean_sq = _bdot((h * h).astype(BF16), jnp.full((h.shape[1],) * 2, 1.0 / h.shape[1], BF16))
        return (h * lax.rsqrt(mean_sq + EPS) * mln_ref[...]).astype(BF16)

    L = CHUNK
    dh = q_ref.shape[1]
    rows = u_sc.shape[1]
    blk, h_idx = pl.program_id(0), pl.program_id(1)
    row = lax.broadcasted_iota(jnp.int32, (L, L), 0)
    col = lax.broadcasted_iota(jnp.int32, (L, L), 1)
    visible = (col <= row, col >= row)
    no_inbound = (not has_init) and nc == 1
    ones_ext = jnp.ones((L, LANES), BF16)

    for d in range(N_DIR):
        rev = d == 1
        ig = gr_ref[0, d, 0, 0]
        lf = jax.nn.log_sigmoid(gr_ref[1, d, 0, 0])
        b = _lane_scan(lf, jnp.add, 0.0, rev)
        tot = jnp.broadcast_to(b[:, 0:1] if rev else b[:, L - 1:L], (rows, L))
        vrow = ig - b
        cm = _lane_scan(vrow, jnp.maximum, -jnp.inf, rev)
        g = tot - b + ig
        gmax = jnp.broadcast_to(jnp.max(g, axis=1, keepdims=True), (rows, L))
        mi_sc[d] = jnp.zeros((rows, L), F32)
        mf_sc[d] = jnp.zeros(mf_sc.shape[1:], F32)
        for jj in range(n_sub):
            if has_init:
                m = jnp.full((1, L), m0_ref[((blk * n_sub + jj) * N_DIR + d) * NH + h_idx], F32)
            else:
                m = jnp.zeros((1, L), F32)
            for c in (range(nc - 1, -1, -1) if rev else range(nc)):
                r = jj * nc + c
                mi_sc[d, r:r + 1, :] = m
                m = jnp.maximum(tot[r:r + 1] + m, gmax[r:r + 1])
            mf_sc[d, jj:jj + 1, :] = m
        m_in = mi_sc[d]
        m_out = jnp.maximum(tot + m_in, gmax)
        mm = jnp.maximum(m_in, cm)
        u_sc[d] = -mm
        wi_sc[d] = jnp.exp(m_in - mm)
        cl_sc[d] = jnp.exp(-(b + mm))
        wk_sc[d] = jnp.exp(g - m_out)
        vr_sc[d] = vrow
        dc_sc[d] = jnp.exp(tot + m_in - m_out)

    def chain(d, r, r0, s_raw):
        one = pl.ds(r, 1)
        qc = q_ref[pl.ds(r0, L), :]
        ktc = kt_ref[r]
        v_ext = jnp.concatenate([v_ref[pl.ds(r0, L), :], ones_ext], axis=1)
        if s_raw is None:
            s_raw = _bdot(qc, ktc)
        tile = jnp.concatenate([u_sc[d, one, :], wi_sc[d, one, :], cl_sc[d, one, :],
                                jnp.zeros((SUBLANES - 3, L), F32)], axis=0)
        cols = tile.T
        u_c, wi_c, cl_c = (cols[:, n:n + 1] for n in range(3))
        s = s_raw * jnp.exp(jnp.where(visible[d], u_c + vr_sc[d, one, :], -jnp.inf))
        acc = _bdot(s.astype(BF16), v_ext)
        if not no_inbound:
            acc = acc + wi_c * _bdot(qc, c_sc[d].astype(BF16))
        rden = 1.0 / jnp.maximum(jnp.abs(acc[:, dh:]), cl_c)
        h = acc[:, :dh] * jnp.concatenate([rden] * (dh // LANES), axis=1)
        kw = (ktc.astype(F32) * wk_sc[d, one, :]).astype(BF16)
        kv = _bdot(kw, v_ext)
        if no_inbound:
            c_sc[d] = kv
        else:
            c_sc[d] = dc_sc[d, one, :][:, 0:1] * c_sc[d] + kv
        return h

    def do_seq(j):
        base = j * (nc * L)
        for d in range(N_DIR):
            if has_init:
                n_rep = jnp.broadcast_to(n0_ref[j, 0, d, pl.ds(h_idx, 1), :], (LANES, dh)).T
                c_sc[d] = jnp.concatenate([c0_ref[j, 0, d, 0], n_rep], axis=1)
            elif not no_inbound:
                c_sc[d] = jnp.zeros(c_sc.shape[1:], F32)

        if nc == 1:
            r0 = pl.multiple_of(base, L)
            s_raw = _bdot(q_ref[pl.ds(r0, L), :], kt_ref[j])
            hn_ref[pl.ds(r0, L), :] = head_norm(chain(0, j, r0, s_raw) + chain(1, j, r0, s_raw))
        else:
            def step(i, first_visit):
                for d in range(N_DIR):
                    c = i if d == 0 else nc - 1 - i
                    r0 = pl.multiple_of(base + c * L, L)
                    h = chain(d, j * nc + c, r0, None)
                    if first_visit:
                        h_sc[pl.ds(r0, L), :] = h
                    else:
                        hn_ref[pl.ds(r0, L), :] = head_norm(h_sc[pl.ds(r0, L), :] + h)

            lax.fori_loop(0, nc // 2, lambda i, carry: (step(i, True), carry)[1], 0,
                          unroll=unroll)
            lax.fori_loop(nc // 2, nc, lambda i, carry: (step(i, False), carry)[1], 0,
                          unroll=unroll)

        if write_state:
            for d in range(N_DIR):
                cout_ref[j, 0, d, 0] = c_sc[d, :, :dh]
                nout_ref[j, d, 0] = c_sc[d, :, dh:].T[0:1, :]
                mout_ref[j, d, 0] = mf_sc[d, pl.ds(j, 1), :][:, :LANES]

    if n_sub == 1:
        do_seq(0)
    else:
        lax.fori_loop(0, n_sub, lambda j, carry: (do_seq(j), carry)[1], 0, unroll=unroll)


def _mlstm_call(z, kt, gates_rows, ml_norm, state, *, n_seq, t_len, n_sub, tok_block0, dh, unroll):
    nc = t_len // CHUNK
    rows = gates_rows.shape[-2]
    assert rows == n_sub * nc and rows % SUBLANES == 0 and (nc == 1 or nc % 2 == 0)
    has_init = state is not None
    write_state = not has_init
    kern = functools.partial(_mlstm_kernel, nc=nc, n_sub=n_sub, has_init=has_init,
                             write_state=write_state, unroll=unroll)
    tb = n_sub * t_len

    def z_spec(group):
        return pl.BlockSpec((tb, dh), lambda b, h: (tok_block0 + b, group * NH + h))

    in_specs = [z_spec(ZQ),
                pl.BlockSpec((rows, dh, CHUNK), lambda b, h: (tok_block0 + b, h, 0)),
                z_spec(ZV),
                pl.BlockSpec((2, N_DIR, 1, 1, rows, CHUNK), lambda b, h: (0, 0, h, b, 0, 0)),
                pl.BlockSpec((1, dh), lambda b, h: (0, h))]
    args = [z, kt, z, gates_rows, ml_norm]
    if has_init:
        c0, n0, m0 = state
        in_specs += [pl.BlockSpec((n_sub, 1, N_DIR, 1, dh, dh), lambda b, h: (b, 0, 0, h, 0, 0)),
                     pl.BlockSpec((n_sub, 1, N_DIR, NH, dh), lambda b, h: (b, 0, 0, 0, 0)),
                     pl.BlockSpec(memory_space=pltpu.SMEM)]
        args += [c0, n0, m0.reshape(-1)]
    out_specs = [pl.BlockSpec((tb, dh), lambda b, h: (b, h))]
    out_shape = [jax.ShapeDtypeStruct((n_seq * t_len, NH * dh), BF16)]
    if write_state:
        out_specs += [pl.BlockSpec((n_sub, 1, N_DIR, 1, dh, dh), lambda b, h: (b, 0, 0, h, 0, 0)),
                      pl.BlockSpec((n_sub, N_DIR, 1, 1, dh), lambda b, h: (b, 0, h, 0, 0)),
                      pl.BlockSpec((n_sub, N_DIR, 1, 1, LANES), lambda b, h: (b, 0, h, 0, 0))]
        out_shape += [jax.ShapeDtypeStruct((n_seq, 1, N_DIR, NH, dh, dh), F32),
                      jax.ShapeDtypeStruct((n_seq, N_DIR, NH, 1, dh), F32),
                      jax.ShapeDtypeStruct((n_seq, N_DIR, NH, 1, LANES), F32)]
    per_row = pltpu.VMEM((N_DIR, rows, CHUNK), F32)
    return pl.pallas_call(
        kern,
        grid=(n_seq // n_sub, NH),
        in_specs=in_specs,
        out_specs=out_specs,
        out_shape=out_shape,
        scratch_shapes=[pltpu.VMEM((N_DIR, dh, dh + LANES), F32)] + [per_row] * 7
        + [pltpu.VMEM((N_DIR, max(SUBLANES, n_sub), CHUNK), F32)]
        + ([pltpu.VMEM((tb, dh), F32)] if nc > 1 else []),
        compiler_params=_params(2),
        name="mlstm_lat" if has_init else "mlstm_ctx",
    )(*args)


def _mixout_kernel(hnc_ref, hnl_ref, zo_ref, zb_ref, zc_ref, zx_ref, zgm_ref, zgs_ref, x1_ref,
                   mod_ref, cw_ref, cb_ref, wml_ref, wsc_ref, wo_ref, x2_ref, *,
                   n_ctx_blocks, blocks_per_seq, ctx_seg):
    is_ctx = pl.program_id(0) < n_ctx_blocks
    gate = _mod_rows(mod_ref, pl.program_id(0), n_ctx_blocks, blocks_per_seq)[5]
    sub = MIX_SUB
    t = lax.broadcasted_iota(jnp.int32, (sub, 1), 0)
    pos = jnp.where(is_ctx, t % ctx_seg, t % GRID_W)
    has_prev = pos != 0
    has_next = pos != jnp.where(is_ctx, ctx_seg - 1, GRID_W - 1)
    for r in range(0, x1_ref.shape[0], sub):
        rows = pl.ds(r, sub)
        hn = jnp.where(is_ctx, hnc_ref[rows, :], hnl_ref[rows, :])
        ml = _bdot(zo_ref[rows, :] * hn, wml_ref[...])

        u = zc_ref[rows, :].astype(F32) * zx_ref[rows, :].astype(F32)
        u_prev = jnp.where(has_prev, pltpu.roll(u, 1, axis=0), 0.0)
        u_next = jnp.where(has_next, pltpu.roll(u, sub - 1, axis=0), 0.0)
        uc = u_prev * cw_ref[0:1] + u * cw_ref[1:2] + u_next * cw_ref[2:3] + cb_ref[...]
        sc = _bdot(zb_ref[rows, :] * uc.astype(BF16), wsc_ref[...])

        y = zgm_ref[rows, :].astype(F32) * ml + zgs_ref[rows, :].astype(F32) * sc
        x2_ref[rows, :] = x1_ref[rows, :] + gate * _bdot(y.astype(BF16), wo_ref[...])


def kernel(x_prompt, x_sample, state_C, state_n, state_m, c, c_ctx, w_mod, b_mod, norm_g,
           ffn1_w1, ffn1_w2, w_in, b_in, conv_w, conv_b, ml_norm, w_ml_out, w_sc_out, w_o,
           ffn2_w1, ffn2_w2, final_norm):
    assert w_mod.shape[0] == 1, "single trunk layer only"
    bp, tp, dm = x_prompt.shape
    bs, ts, _ = x_sample.shape
    n_ctx, n_lat = bp * tp, bs * ts
    n_tok = n_ctx + n_lat
    d_ff = ffn1_w2.shape[1]
    ml_dim = w_ml_out.shape[1]
    sc_dim = w_sc_out.shape[1]
    dh = ml_dim // NH
    n_gate = 2 * N_DIR * NH
    assert ml_dim == dm and sc_dim == dm and dh % 256 == 0 and dm == TN_IN
    assert w_in.shape[2] == 9 * dm + n_gate and n_gate < LANES
    assert n_ctx % TM_FFN == 0 and ts % TM_FFN == 0 and n_tok % TM_IN == 0
    assert n_ctx % TM_MIX == 0 and ts % TM_MIX == 0 and MIX_SUB % tp == 0 and MIX_SUB % GRID_W == 0
    assert TM_K % CHUNK == 0 and n_tok % TM_K == 0
    assert tp == CHUNK and ts % (2 * CHUNK) == 0 and bp % CTX_SEQS == 0 and n_ctx % ts == 0
    n_cond = 1 + bs

    cond = jnp.concatenate([c_ctx[None, :], c], axis=0)
    condT = jnp.pad(cond.T, ((0, 0), (0, SUBLANES - n_cond)))
    mod_spec = _resident((N_MOD, SUBLANES, dm))

    assert w_mod.shape[2] == N_MOD * dm and N_MOD <= FFN_WSTEPS
    w_in0, b_in0 = w_in[0], b_in[0]
    gate_lo = 4 * ml_dim
    ncb = n_ctx // TM_FFN
    nb = n_tok // TM_FFN
    chunks = _ff_chunks(d_ff)
    wc1, wc2 = 2 * d_ff // FFN_WSTEPS, d_ff // FFN_WSTEPS
    assert wc1 * FFN_WSTEPS == 2 * d_ff and wc1 % LANES == 0
    assert wc2 * FFN_WSTEPS == d_ff and wc2 % (2 * SUBLANES) == 0

    def tok(i):
        return jnp.maximum(i - FFN_WSTEPS, 0)

    def wstep(i):
        return jnp.minimum(i, FFN_WSTEPS - 1)

    ffn_w_specs = [pl.BlockSpec((dm, wc1), lambda i: (0, wstep(i))),
                   pl.BlockSpec((wc2, dm), lambda i: (wstep(i), 0))]
    ffn_w_scratch = [pltpu.VMEM((dm, 2 * d_ff), BF16), pltpu.VMEM((d_ff, dm), BF16)]
    tok_spec = pl.BlockSpec((TM_FFN, dm), lambda i: (tok(i), 0))
    x1, h2, mod = pl.pallas_call(
        functools.partial(_ffn1_kernel, chunks=chunks, n_ctx_blocks=ncb,
                          blocks_per_seq=ts // TM_FFN, n_cond=n_cond),
        grid=(FFN_WSTEPS + nb,),
        in_specs=[pl.BlockSpec((TM_FFN, dm), lambda i: (jnp.minimum(tok(i), ncb - 1), 0)),
                  pl.BlockSpec((TM_FFN, dm), lambda i: (jnp.maximum(tok(i) - ncb, 0), 0)),
                  _resident((dm, SUBLANES)),
                  pl.BlockSpec((dm, dm), lambda i: (0, jnp.minimum(i, N_MOD - 1))),
                  _resident((N_MOD, dm)), _resident((3, dm))] + ffn_w_specs,
        out_specs=[tok_spec, tok_spec,
                   pl.BlockSpec((N_MOD, SUBLANES, dm), lambda i: (0, 0, 0))],
        out_shape=[jax.ShapeDtypeStruct((n_tok, dm), F32), jax.ShapeDtypeStruct((n_tok, dm), BF16),
                   jax.ShapeDtypeStruct((N_MOD, SUBLANES, dm), F32)],
        scratch_shapes=ffn_w_scratch,
        compiler_params=_params(1),
        name="ffn1",
    )(x_prompt.reshape(n_ctx, dm), x_sample.reshape(n_lat, dm), condT, w_mod[0],
      b_mod[0].reshape(N_MOD, dm), norm_g[0], ffn1_w1[0], ffn1_w2[0])

    w_in_t = w_in0.T
    n_lead = gate_lo // TN_IN - 1
    b_main = jnp.concatenate([b_in0[:ml_dim], b_in0[2 * ml_dim:gate_lo],
                              b_in0[gate_lo + n_gate:]])[None, :]
    n_main = b_main.shape[1]

    def w_block(j):
        return jnp.where(j == 0, 0, j + 1)

    z = pl.pallas_call(
        functools.partial(_inproj_kernel, n_lead=n_lead, shift=n_gate),
        grid=(n_main // TN_IN, n_tok // TM_IN),
        in_specs=[pl.BlockSpec((TM_IN, dm), lambda j, i: (i, 0)),
                  pl.BlockSpec((TN_IN, dm), lambda j, i: (w_block(j), 0)),
                  pl.BlockSpec((TN_IN, dm), lambda j, i: (w_block(j) + 1, 0)),
                  pl.BlockSpec((1, TN_IN), lambda j, i: (0, j))],
        out_specs=pl.BlockSpec((TM_IN, TN_IN), lambda j, i: (i, j)),
        out_shape=jax.ShapeDtypeStruct((n_tok, n_main), BF16),
        scratch_shapes=[pltpu.VMEM((TN_IN, dm), BF16)],
        compiler_params=_params(2),
        name="inproj",
    )(h2, w_in_t, w_in_t, b_main)

    kscale = dh ** -0.5
    w_kg = jnp.concatenate([w_in_t[ml_dim:2 * ml_dim] * kscale,
                            w_in_t[gate_lo:gate_lo + n_gate]], axis=0).astype(BF16)
    b_kg = jnp.concatenate([b_in0[ml_dim:2 * ml_dim] * kscale,
                            b_in0[gate_lo:gate_lo + n_gate]])[:, None]
    kt, gates_t = pl.pallas_call(
        _kproj_kernel,
        grid=(n_tok // TM_K,),
        in_specs=[pl.BlockSpec((TM_K, dm), lambda i: (i, 0)), _resident((dm + n_gate, dm)),
                  _resident((dm + n_gate, 1))],
        out_specs=[pl.BlockSpec((TM_K // CHUNK, dm, CHUNK), lambda i: (i, 0, 0)),
                   pl.BlockSpec((n_gate, TM_K), lambda i: (0, i))],
        out_shape=[jax.ShapeDtypeStruct((n_tok // CHUNK, dm, CHUNK), BF16),
                   jax.ShapeDtypeStruct((n_gate, n_tok), F32)],
        compiler_params=_params(1),
        name="kproj",
    )(h2, w_kg, b_kg)

    g_rows = gates_t.reshape(2, N_DIR, NH, n_tok)
    rows_ctx = CTX_SEQS * (tp // CHUNK)
    hn_ctx, new_c, new_n, new_m = _mlstm_call(
        z, kt, g_rows[..., :n_ctx].reshape(2, N_DIR, NH, bp // CTX_SEQS, rows_ctx, CHUNK),
        ml_norm, None,
        n_seq=bp, t_len=tp, n_sub=CTX_SEQS, tok_block0=0, dh=dh, unroll=4)
    (hn_lat,) = _mlstm_call(
        z, kt, g_rows[..., n_ctx:].reshape(2, N_DIR, NH, bs, ts // CHUNK, CHUNK),
        ml_norm, (state_C, state_n, state_m),
        n_seq=bs, t_len=ts, n_sub=1, tok_block0=n_ctx // ts, dh=dh, unroll=4)

    ncb_m = n_ctx // TM_MIX
    tokm = pl.BlockSpec((TM_MIX, dm), lambda i: (i, 0))

    def zcol(j):
        return pl.BlockSpec((TM_MIX, dm), lambda i: (i, j))

    x2 = pl.pallas_call(
        functools.partial(_mixout_kernel, n_ctx_blocks=ncb_m, blocks_per_seq=ts // TM_MIX,
                          ctx_seg=tp),
        grid=(n_tok // TM_MIX,),
        in_specs=[pl.BlockSpec((TM_MIX, dm), lambda i: (jnp.minimum(i, ncb_m - 1), 0)),
                  pl.BlockSpec((TM_MIX, dm), lambda i: (jnp.maximum(i - ncb_m, 0), 0)),
                  zcol(ZO), zcol(ZB), zcol(ZC), zcol(ZX), zcol(ZGM), zcol(ZGS), tokm,
                  mod_spec, _resident((3, dm)), _resident((1, dm)),
                  _resident((dm, dm)), _resident((dm, dm)), _resident((dm, dm))],
        out_specs=tokm,
        out_shape=jax.ShapeDtypeStruct((n_tok, dm), F32),
        compiler_params=_params(1),
        name="mixout",
    )(hn_ctx, hn_lat, z, z, z, z, z, z, x1, mod, conv_w[0],
      conv_b[0][None, :], w_ml_out[0].astype(BF16), w_sc_out[0].astype(BF16), w_o[0].astype(BF16))

    y_p, y_s = pl.pallas_call(
        functools.partial(_ffn2_kernel, chunks=chunks, n_ctx_blocks=ncb,
                          blocks_per_seq=ts // TM_FFN),
        grid=(FFN_WSTEPS + nb,),
        in_specs=[tok_spec, mod_spec, _resident((3, dm)), _resident((1, dm))]
        + ffn_w_specs,
        out_specs=[pl.BlockSpec((TM_FFN, dm), lambda i: (jnp.minimum(tok(i), ncb - 1), 0)),
                   pl.BlockSpec((TM_FFN, dm), lambda i: (jnp.maximum(tok(i) - ncb, 0), 0))],
        out_shape=[jax.ShapeDtypeStruct((n_ctx, dm), F32), jax.ShapeDtypeStruct((n_lat, dm), F32)],
        scratch_shapes=ffn_w_scratch,
        compiler_params=_params(1),
        name="ffn2",
    )(x2, mod, norm_g[0], final_norm[None, :], ffn2_w1[0], ffn2_w2[0])

    return (y_p.reshape(bp, tp, dm), y_s.reshape(bs, ts, dm), new_c,
            new_n.reshape(bp, 1, N_DIR, NH, dh), new_m[..., 0, 0].reshape(bp, 1, N_DIR, NH))
```

```python
import functools

import jax
import jax.numpy as jnp
from jax import lax
from jax.experimental import pallas as pl
from jax.experimental.pallas import tpu as pltpu

F32 = jnp.float32
BF16 = jnp.bfloat16

NH = 4
N_DIR = 2
N_MOD = 9
GRID_W = 64
EPS = 1e-6
LANES = 128
SUBLANES = 8
VMEM_LIMIT = 56 * 1024 * 1024

TM_FFN = 512
FFN_SUB = 256
FFN_WSTEPS = 11
TM_MIX = 512
MIX_SUB = 256
TM_IN = 2048
TN_IN = 1024
TM_K = 1024
CHUNK = 256
CTX_SEQS = 8
ZQ, ZV, ZO, ZB, ZC, ZX, ZGM, ZGS = range(8)
SIG_GROUPS = (ZO, ZGM, ZGS)


def _params(n_axes):
    return pltpu.CompilerParams(dimension_semantics=("arbitrary",) * n_axes,
                                vmem_limit_bytes=VMEM_LIMIT)


def _resident(shape):
    zeros = (0,) * len(shape)
    return pl.BlockSpec(shape, lambda *_: zeros, pipeline_mode=pl.Buffered(1))


def _rms(x, g):
    return x * lax.rsqrt(jnp.mean(x * x, axis=-1, keepdims=True) + EPS) * g


def _bdot(a, b):
    return jnp.dot(a, b, preferred_element_type=F32)


def _mod_rows(mod_ref, t, n_ctx_blocks, blocks_per_seq):
    r = jnp.where(t < n_ctx_blocks, 0, 1 + (t - n_ctx_blocks) // blocks_per_seq)
    return [mod_ref[k, pl.ds(r, 1), :] for k in range(N_MOD)]


def _swiglu_residual(x, mod, g_norm, w1_ref, w2_ref, chunks, row0):
    d_ff = w2_ref.shape[0]
    h = (_rms(x, g_norm) * (1.0 + mod[row0 + 1]) + mod[row0]).astype(BF16)
    acc = None
    for lo, hi in chunks:
        a = _bdot(h, w1_ref[:, lo:hi])
        b = _bdot(h, w1_ref[:, d_ff + lo:d_ff + hi])
        part = _bdot((jax.nn.silu(a) * b).astype(BF16), w2_ref[lo:hi, :])
        acc = part if acc is None else acc + part
    return x + (0.5 * mod[row0 + 2]) * acc


def _row_parts(tm):
    return [pl.ds(r, FFN_SUB) for r in range(0, tm, FFN_SUB)]


def _stage_weights(i, w1f_ref, w2f_ref, w1_sc, w2_sc, also=None):
    c1, c2 = w1f_ref.shape[1], w2f_ref.shape[0]
    for c in range(FFN_WSTEPS):
        @pl.when(i == c)
        def _(c=c):
            w1_sc[:, c * c1:(c + 1) * c1] = w1f_ref[...].astype(BF16)
            w2_sc[c * c2:(c + 1) * c2, :] = w2f_ref[...].astype(BF16)
            if also is not None:
                also(c)


def _ffn1_kernel(xp_ref, xs_ref, condT_ref, wmod_ref, bmod_ref, g_ref, w1f_ref, w2f_ref,
                 x1_ref, h2_ref, mod_ref, w1_sc, w2_sc, *,
                 chunks, n_ctx_blocks, blocks_per_seq, n_cond):
    i = pl.program_id(0)

    def mod_chunk(c):
        if c < N_MOD:
            s = jax.nn.silu(condT_ref[...])
            w = wmod_ref[...]
            rows = [jnp.sum(w * s[:, r:r + 1], axis=0, keepdims=True) for r in range(n_cond)]
            rows.append(jnp.zeros((SUBLANES - n_cond, w.shape[1]), F32))
            mod_ref[c] = jnp.concatenate(rows, axis=0) + bmod_ref[c:c + 1, :]

    _stage_weights(i, w1f_ref, w2f_ref, w1_sc, w2_sc, also=mod_chunk)

    @pl.when(i >= FFN_WSTEPS)
    def _():
        t = i - FFN_WSTEPS
        is_ctx = t < n_ctx_blocks
        mod = _mod_rows(mod_ref, t, n_ctx_blocks, blocks_per_seq)
        for rows in _row_parts(x1_ref.shape[0]):
            x = jnp.where(is_ctx, xp_ref[rows, :], xs_ref[rows, :])
            x1 = _swiglu_residual(x, mod, g_ref[0:1], w1_sc, w2_sc, chunks, 0)
            x1_ref[rows, :] = x1
            h2_ref[rows, :] = (_rms(x1, g_ref[1:2]) * (1.0 + mod[4]) + mod[3]).astype(BF16)


def _ffn2_kernel(x_ref, mod_ref, g_ref, fin_ref, w1f_ref, w2f_ref, yp_ref, ys_ref, w1_sc, w2_sc, *,
                 chunks, n_ctx_blocks, blocks_per_seq):
    i = pl.program_id(0)
    _stage_weights(i, w1f_ref, w2f_ref, w1_sc, w2_sc)

    @pl.when(i >= FFN_WSTEPS)
    def _():
        mod = _mod_rows(mod_ref, i - FFN_WSTEPS, n_ctx_blocks, blocks_per_seq)
        y = jnp.concatenate(
            [_rms(_swiglu_residual(x_ref[rows, :], mod, g_ref[2:3], w1_sc, w2_sc, chunks, 6),
                  fin_ref[...]) for rows in _row_parts(x_ref.shape[0])], axis=0)

        @pl.when(i - FFN_WSTEPS < n_ctx_blocks)
        def _():
            yp_ref[...] = y

        @pl.when(i - FFN_WSTEPS >= n_ctx_blocks)
        def _():
            ys_ref[...] = y


def _ff_chunks(d_ff):
    tile = 256
    half = max(tile, (d_ff // 2 + tile - 1) // tile * tile)
    return ((0, half), (half, d_ff)) if half < d_ff else ((0, d_ff),)


def _inproj_kernel(h_ref, wa_ref, wb_ref, b_ref, z_ref, w_sc, *, n_lead, shift):
    j, i = pl.program_id(0), pl.program_id(1)

    @pl.when(jnp.logical_and(i == 0, j < n_lead))
    def _():
        w_sc[...] = wa_ref[...].astype(BF16)

    @pl.when(jnp.logical_and(i == 0, j >= n_lead))
    def _():
        w_sc[...] = jnp.concatenate([wa_ref[shift:, :], wb_ref[:shift, :]], axis=0).astype(BF16)

    def project():
        return lax.dot_general(h_ref[...], w_sc[...], (((1,), (1,)), ((), ())),
                               preferred_element_type=F32) + b_ref[...]

    is_sig = functools.reduce(jnp.logical_or, [j == g for g in SIG_GROUPS])

    @pl.when(is_sig)
    def _():
        z_ref[...] = (0.5 * jnp.tanh(0.5 * project()) + 0.5).astype(BF16)

    @pl.when(jnp.logical_not(is_sig))
    def _():
        z_ref[...] = project().astype(BF16)


def _kproj_kernel(h_ref, wt_ref, b_ref, kt_ref, gt_ref):
    dm = kt_ref.shape[1]
    full = lax.dot_general(wt_ref[...], h_ref[...], (((1,), (1,)), ((), ())),
                           preferred_element_type=F32) + b_ref[...]
    for c in range(kt_ref.shape[0]):
        kt_ref[c] = full[:dm, c * CHUNK:(c + 1) * CHUNK].astype(BF16)
    gt_ref[...] = full[dm:, :]


def _lane_scan(x, op, fill, reverse):
    n = x.shape[1]
    lane = lax.broadcasted_iota(jnp.int32, x.shape, 1)
    sh = 1
    while sh < n:
        if reverse:
            nb = jnp.where(lane < n - sh, pltpu.roll(x, n - sh, axis=1), fill)
        else:
            nb = jnp.where(lane >= sh, pltpu.roll(x, sh, axis=1), fill)
        x = op(x, nb)
        sh *= 2
    return x


def _mlstm_kernel(*refs, nc, n_sub, has_init, write_state, unroll):
    it = iter(refs)
    q_ref, kt_ref, v_ref, gr_ref, mln_ref = (next(it) for _ in range(5))
    if has_init:
        c0_ref, n0_ref, m0_ref = next(it), next(it), next(it)
    hn_ref = next(it)
    if write_state:
        cout_ref, nout_ref, mout_ref = next(it), next(it), next(it)
    c_sc, u_sc, wi_sc, cl_sc, wk_sc, vr_sc, dc_sc, mi_sc, mf_sc = (next(it) for _ in range(9))
    if nc > 1:
        h_sc = next(it)

    def head_norm(h):
        mean_sq = _bdot((h * h).astype(BF16), jnp.full((h.shape[1],) * 2, 1.0 / h.shape[1], BF16))
        return (h * lax.rsqrt(mean_sq + EPS) * mln_ref[...]).astype(BF16)

    L = CHUNK
    dh = q_ref.shape[1]
    rows = u_sc.shape[1]
    blk, h_idx = pl.program_id(0), pl.program_id(1)
    row = lax.broadcasted_iota(jnp.int32, (L, L), 0)
    col = lax.broadcasted_iota(jnp.int32, (L, L), 1)
    visible = (col <= row, col >= row)
    no_inbound = (not has_init) and nc == 1
    ones_ext = jnp.ones((L, LANES), BF16)

    for d in range(N_DIR):
        rev = d == 1
        ig = gr_ref[0, d, 0, 0]
        lf = jax.nn.log_sigmoid(gr_ref[1, d, 0, 0])
        b = _lane_scan(lf, jnp.add, 0.0, rev)
        tot = jnp.broadcast_to(b[:, 0:1] if rev else b[:, L - 1:L], (rows, L))
        vrow = ig - b
        cm = _lane_scan(vrow, jnp.maximum, -jnp.inf, rev)
        g = tot - b + ig
        gmax = jnp.broadcast_to(jnp.max(g, axis=1, keepdims=True), (rows, L))
        mi_sc[d] = jnp.zeros((rows, L), F32)
        mf_sc[d] = jnp.zeros(mf_sc.shape[1:], F32)
        for jj in range(n_sub):
            if has_init:
                m = jnp.full((1, L), m0_ref[((blk * n_sub + jj) * N_DIR + d) * NH + h_idx], F32)
            else:
                m = jnp.zeros((1, L), F32)
            for c in (range(nc - 1, -1, -1) if rev else range(nc)):
                r = jj * nc + c
                mi_sc[d, r:r + 1, :] = m
                m = jnp.maximum(tot[r:r + 1] + m, gmax[r:r + 1])
            mf_sc[d, jj:jj + 1, :] = m
        m_in = mi_sc[d]
        m_out = jnp.maximum(tot + m_in, gmax)
        mm = jnp.maximum(m_in, cm)
        u_sc[d] = -mm
        wi_sc[d] = jnp.exp(m_in - mm)
        cl_sc[d] = jnp.exp(-(b + mm))
        wk_sc[d] = jnp.exp(g - m_out)
        vr_sc[d] = vrow
        dc_sc[d] = jnp.exp(tot + m_in - m_out)

    def chain(d, r, r0, s_raw):
        one = pl.ds(r, 1)
        qc = q_ref[pl.ds(r0, L), :]
        ktc = kt_ref[r]
        v_ext = jnp.concatenate([v_ref[pl.ds(r0, L), :], ones_ext], axis=1)
        if s_raw is None:
            s_raw = _bdot(qc, ktc)
        def per_query(row_ref, width):
            col = jnp.broadcast_to(row_ref[d, one, :], (LANES, L)).T
            return jnp.concatenate([col] * (width // LANES), axis=1)

        s = s_raw * jnp.exp(jnp.where(visible[d], per_query(u_sc, L) + vr_sc[d, one, :], -jnp.inf))
        acc = _bdot(s.astype(BF16), v_ext)
        if not no_inbound:
            acc = acc + per_query(wi_sc, dh + LANES) * _bdot(qc, c_sc[d].astype(BF16))
        rden = 1.0 / jnp.maximum(jnp.abs(acc[:, dh:]), per_query(cl_sc, LANES))
        h = acc[:, :dh] * jnp.concatenate([rden] * (dh // LANES), axis=1)
        kw = (ktc.astype(F32) * wk_sc[d, one, :]).astype(BF16)
        kv = _bdot(kw, v_ext)
        if no_inbound:
            c_sc[d] = kv
        else:
            c_sc[d] = dc_sc[d, one, :][:, 0:1] * c_sc[d] + kv
        return h

    def do_seq(j):
        base = j * (nc * L)
        for d in range(N_DIR):
            if has_init:
                n_rep = jnp.broadcast_to(n0_ref[j, 0, d, pl.ds(h_idx, 1), :], (LANES, dh)).T
                c_sc[d] = jnp.concatenate([c0_ref[j, 0, d, 0], n_rep], axis=1)
            elif not no_inbound:
                c_sc[d] = jnp.zeros(c_sc.shape[1:], F32)

        if nc == 1:
            r0 = pl.multiple_of(base, L)
            s_raw = _bdot(q_ref[pl.ds(r0, L), :], kt_ref[j])
            hn_ref[pl.ds(r0, L), :] = head_norm(chain(0, j, r0, s_raw) + chain(1, j, r0, s_raw))
        else:
            def step(i, first_visit):
                for d in range(N_DIR):
                    c = i if d == 0 else nc - 1 - i
                    r0 = pl.multiple_of(base + c * L, L)
                    h = chain(d, j * nc + c, r0, None)
                    if first_visit:
                        h_sc[pl.ds(r0, L), :] = h
                    else:
                        hn_ref[pl.ds(r0, L), :] = head_norm(h_sc[pl.ds(r0, L), :] + h)

            lax.fori_loop(0, nc // 2, lambda i, carry: (step(i, True), carry)[1], 0,
                          unroll=unroll)
            lax.fori_loop(nc // 2, nc, lambda i, carry: (step(i, False), carry)[1], 0,
                          unroll=unroll)

        if write_state:
            for d in range(N_DIR):
                cout_ref[j, 0, d, 0] = c_sc[d, :, :dh]
                nout_ref[j, d, 0] = c_sc[d, :, dh:].T[0:1, :]
                mout_ref[j, d, 0] = mf_sc[d, pl.ds(j, 1), :][:, :LANES]

    if n_sub == 1:
        do_seq(0)
    else:
        lax.fori_loop(0, n_sub, lambda j, carry: (do_seq(j), carry)[1], 0, unroll=unroll)


def _mlstm_call(z, kt, gates_rows, ml_norm, state, *, n_seq, t_len, n_sub, tok_block0, dh, unroll):
    nc = t_len // CHUNK
    rows = gates_rows.shape[-2]
    assert rows == n_sub * nc and rows % SUBLANES == 0 and (nc == 1 or nc % 2 == 0)
    has_init = state is not None
    write_state = not has_init
    kern = functools.partial(_mlstm_kernel, nc=nc, n_sub=n_sub, has_init=has_init,
                             write_state=write_state, unroll=unroll)
    tb = n_sub * t_len

    def z_spec(group):
        return pl.BlockSpec((tb, dh), lambda b, h: (tok_block0 + b, group * NH + h))

    in_specs = [z_spec(ZQ),
                pl.BlockSpec((rows, dh, CHUNK), lambda b, h: (tok_block0 + b, h, 0)),
                z_spec(ZV),
                pl.BlockSpec((2, N_DIR, 1, 1, rows, CHUNK), lambda b, h: (0, 0, h, b, 0, 0)),
                pl.BlockSpec((1, dh), lambda b, h: (0, h))]
    args = [z, kt, z, gates_rows, ml_norm]
    if has_init:
        c0, n0, m0 = state
        in_specs += [pl.BlockSpec((n_sub, 1, N_DIR, 1, dh, dh), lambda b, h: (b, 0, 0, h, 0, 0)),
                     pl.BlockSpec((n_sub, 1, N_DIR, NH, dh), lambda b, h: (b, 0, 0, 0, 0)),
                     pl.BlockSpec(memory_space=pltpu.SMEM)]
        args += [c0, n0, m0.reshape(-1)]
    out_specs = [pl.BlockSpec((tb, dh), lambda b, h: (b, h))]
    out_shape = [jax.ShapeDtypeStruct((n_seq * t_len, NH * dh), BF16)]
    if write_state:
        out_specs += [pl.BlockSpec((n_sub, 1, N_DIR, 1, dh, dh), lambda b, h: (b, 0, 0, h, 0, 0)),
                      pl.BlockSpec((n_sub, N_DIR, 1, 1, dh), lambda b, h: (b, 0, h, 0, 0)),
                      pl.BlockSpec((n_sub, N_DIR, 1, 1, LANES), lambda b, h: (b, 0, h, 0, 0))]
        out_shape += [jax.ShapeDtypeStruct((n_seq, 1, N_DIR, NH, dh, dh), F32),
                      jax.ShapeDtypeStruct((n_seq, N_DIR, NH, 1, dh), F32),
                      jax.ShapeDtypeStruct((n_seq, N_DIR, NH, 1, LANES), F32)]
    per_row = pltpu.VMEM((N_DIR, rows, CHUNK), F32)
    return pl.pallas_call(
        kern,
        grid=(n_seq // n_sub, NH),
        in_specs=in_specs,
        out_specs=out_specs,
        out_shape=out_shape,
        scratch_shapes=[pltpu.VMEM((N_DIR, dh, dh + LANES), F32)] + [per_row] * 7
        + [pltpu.VMEM((N_DIR, max(SUBLANES, n_sub), CHUNK), F32)]
        + ([pltpu.VMEM((tb, dh), F32)] if nc > 1 else []),
        compiler_params=_params(2),
        name="mlstm_lat" if has_init else "mlstm_ctx",
    )(*args)


def _mixout_kernel(hnc_ref, hnl_ref, zo_ref, zb_ref, zc_ref, zx_ref, zgm_ref, zgs_ref, x1_ref,
                   mod_ref, cw_ref, cb_ref, wml_ref, wsc_ref, wo_ref, x2_ref, *,
                   n_ctx_blocks, blocks_per_seq, ctx_seg):
    is_ctx = pl.program_id(0) < n_ctx_blocks
    gate = _mod_rows(mod_ref, pl.program_id(0), n_ctx_blocks, blocks_per_seq)[5]
    sub = MIX_SUB
    t = lax.broadcasted_iota(jnp.int32, (sub, 1), 0)
    pos = jnp.where(is_ctx, t % ctx_seg, t % GRID_W)
    has_prev = pos != 0
    has_next = pos != jnp.where(is_ctx, ctx_seg - 1, GRID_W - 1)
    for r in range(0, x1_ref.shape[0], sub):
        rows = pl.ds(r, sub)
        hn = jnp.where(is_ctx, hnc_ref[rows, :], hnl_ref[rows, :])
        ml = _bdot(zo_ref[rows, :] * hn, wml_ref[...])

        u = zc_ref[rows, :].astype(F32) * zx_ref[rows, :].astype(F32)
        u_prev = jnp.where(has_prev, pltpu.roll(u, 1, axis=0), 0.0)
        u_next = jnp.where(has_next, pltpu.roll(u, sub - 1, axis=0), 0.0)
        uc = u_prev * cw_ref[0:1] + u * cw_ref[1:2] + u_next * cw_ref[2:3] + cb_ref[...]
        sc = _bdot(zb_ref[rows, :] * uc.astype(BF16), wsc_ref[...])

        y = zgm_ref[rows, :].astype(F32) * ml + zgs_ref[rows, :].astype(F32) * sc
        x2_ref[rows, :] = x1_ref[rows, :] + gate * _bdot(y.astype(BF16), wo_ref[...])


def kernel(x_prompt, x_sample, state_C, state_n, state_m, c, c_ctx, w_mod, b_mod, norm_g,
           ffn1_w1, ffn1_w2, w_in, b_in, conv_w, conv_b, ml_norm, w_ml_out, w_sc_out, w_o,
           ffn2_w1, ffn2_w2, final_norm):
    assert w_mod.shape[0] == 1, "single trunk layer only"
    bp, tp, dm = x_prompt.shape
    bs, ts, _ = x_sample.shape
    n_ctx, n_lat = bp * tp, bs * ts
    n_tok = n_ctx + n_lat
    d_ff = ffn1_w2.shape[1]
    ml_dim = w_ml_out.shape[1]
    sc_dim = w_sc_out.shape[1]
    dh = ml_dim // NH
    n_gate = 2 * N_DIR * NH
    assert ml_dim == dm and sc_dim == dm and dh % 256 == 0 and dm == TN_IN
    assert w_in.shape[2] == 9 * dm + n_gate and n_gate < LANES
    assert n_ctx % TM_FFN == 0 and ts % TM_FFN == 0 and n_tok % TM_IN == 0
    assert n_ctx % TM_MIX == 0 and ts % TM_MIX == 0 and MIX_SUB % tp == 0 and MIX_SUB % GRID_W == 0
    assert TM_K % CHUNK == 0 and n_tok % TM_K == 0
    assert tp == CHUNK and ts % (2 * CHUNK) == 0 and bp % CTX_SEQS == 0 and n_ctx % ts == 0
    n_cond = 1 + bs

    cond = jnp.concatenate([c_ctx[None, :], c], axis=0)
    condT = jnp.pad(cond.T, ((0, 0), (0, SUBLANES - n_cond)))
    mod_spec = _resident((N_MOD, SUBLANES, dm))

    assert w_mod.shape[2] == N_MOD * dm and N_MOD <= FFN_WSTEPS
    w_in0, b_in0 = w_in[0], b_in[0]
    gate_lo = 4 * ml_dim
    ncb = n_ctx // TM_FFN
    nb = n_tok // TM_FFN
    chunks = _ff_chunks(d_ff)
    wc1, wc2 = 2 * d_ff // FFN_WSTEPS, d_ff // FFN_WSTEPS
    assert wc1 * FFN_WSTEPS == 2 * d_ff and wc1 % LANES == 0
    assert wc2 * FFN_WSTEPS == d_ff and wc2 % (2 * SUBLANES) == 0

    def tok(i):
        return jnp.maximum(i - FFN_WSTEPS, 0)

    def wstep(i):
        return jnp.minimum(i, FFN_WSTEPS - 1)

    ffn_w_specs = [pl.BlockSpec((dm, wc1), lambda i: (0, wstep(i))),
                   pl.BlockSpec((wc2, dm), lambda i: (wstep(i), 0))]
    ffn_w_scratch = [pltpu.VMEM((dm, 2 * d_ff), BF16), pltpu.VMEM((d_ff, dm), BF16)]
    tok_spec = pl.BlockSpec((TM_FFN, dm), lambda i: (tok(i), 0))
    x1, h2, mod = pl.pallas_call(
        functools.partial(_ffn1_kernel, chunks=chunks, n_ctx_blocks=ncb,
                          blocks_per_seq=ts // TM_FFN, n_cond=n_cond),
        grid=(FFN_WSTEPS + nb,),
        in_specs=[pl.BlockSpec((TM_FFN, dm), lambda i: (jnp.minimum(tok(i), ncb - 1), 0)),
                  pl.BlockSpec((TM_FFN, dm), lambda i: (jnp.maximum(tok(i) - ncb, 0), 0)),
                  _resident((dm, SUBLANES)),
                  pl.BlockSpec((dm, dm), lambda i: (0, jnp.minimum(i, N_MOD - 1))),
                  _resident((N_MOD, dm)), _resident((3, dm))] + ffn_w_specs,
        out_specs=[tok_spec, tok_spec,
                   pl.BlockSpec((N_MOD, SUBLANES, dm), lambda i: (0, 0, 0))],
        out_shape=[jax.ShapeDtypeStruct((n_tok, dm), F32), jax.ShapeDtypeStruct((n_tok, dm), BF16),
                   jax.ShapeDtypeStruct((N_MOD, SUBLANES, dm), F32)],
        scratch_shapes=ffn_w_scratch,
        compiler_params=_params(1),
        name="ffn1",
    )(x_prompt.reshape(n_ctx, dm), x_sample.reshape(n_lat, dm), condT, w_mod[0],
      b_mod[0].reshape(N_MOD, dm), norm_g[0], ffn1_w1[0], ffn1_w2[0])

    w_in_t = w_in0.T
    n_lead = gate_lo // TN_IN - 1
    b_main = jnp.concatenate([b_in0[:ml_dim], b_in0[2 * ml_dim:gate_lo],
                              b_in0[gate_lo + n_gate:]])[None, :]
    n_main = b_main.shape[1]

    def w_block(j):
        return jnp.where(j == 0, 0, j + 1)

    z = pl.pallas_call(
        functools.partial(_inproj_kernel, n_lead=n_lead, shift=n_gate),
        grid=(n_main // TN_IN, n_tok // TM_IN),
        in_specs=[pl.BlockSpec((TM_IN, dm), lambda j, i: (i, 0)),
                  pl.BlockSpec((TN_IN, dm), lambda j, i: (w_block(j), 0)),
                  pl.BlockSpec((TN_IN, dm), lambda j, i: (w_block(j) + 1, 0)),
                  pl.BlockSpec((1, TN_IN), lambda j, i: (0, j))],
        out_specs=pl.BlockSpec((TM_IN, TN_IN), lambda j, i: (i, j)),
        out_shape=jax.ShapeDtypeStruct((n_tok, n_main), BF16),
        scratch_shapes=[pltpu.VMEM((TN_IN, dm), BF16)],
        compiler_params=_params(2),
        name="inproj",
    )(h2, w_in_t, w_in_t, b_main)

    kscale = dh ** -0.5
    w_kg = jnp.concatenate([w_in_t[ml_dim:2 * ml_dim] * kscale,
                            w_in_t[gate_lo:gate_lo + n_gate]], axis=0).astype(BF16)
    b_kg = jnp.concatenate([b_in0[ml_dim:2 * ml_dim] * kscale,
                            b_in0[gate_lo:gate_lo + n_gate]])[:, None]
    kt, gates_t = pl.pallas_call(
        _kproj_kernel,
        grid=(n_tok // TM_K,),
        in_specs=[pl.BlockSpec((TM_K, dm), lambda i: (i, 0)), _resident((dm + n_gate, dm)),
                  _resident((dm + n_gate, 1))],
        out_specs=[pl.BlockSpec((TM_K // CHUNK, dm, CHUNK), lambda i: (i, 0, 0)),
                   pl.BlockSpec((n_gate, TM_K), lambda i: (0, i))],
        out_shape=[jax.ShapeDtypeStruct((n_tok // CHUNK, dm, CHUNK), BF16),
                   jax.ShapeDtypeStruct((n_gate, n_tok), F32)],
        compiler_params=_params(1),
        name="kproj",
    )(h2, w_kg, b_kg)

    g_rows = gates_t.reshape(2, N_DIR, NH, n_tok)
    rows_ctx = CTX_SEQS * (tp // CHUNK)
    hn_ctx, new_c, new_n, new_m = _mlstm_call(
        z, kt, g_rows[..., :n_ctx].reshape(2, N_DIR, NH, bp // CTX_SEQS, rows_ctx, CHUNK),
        ml_norm, None,
        n_seq=bp, t_len=tp, n_sub=CTX_SEQS, tok_block0=0, dh=dh, unroll=4)
    (hn_lat,) = _mlstm_call(
        z, kt, g_rows[..., n_ctx:].reshape(2, N_DIR, NH, bs, ts // CHUNK, CHUNK),
        ml_norm, (state_C, state_n, state_m),
        n_seq=bs, t_len=ts, n_sub=1, tok_block0=n_ctx // ts, dh=dh, unroll=4)

    ncb_m = n_ctx // TM_MIX
    tokm = pl.BlockSpec((TM_MIX, dm), lambda i: (i, 0))

    def zcol(j):
        return pl.BlockSpec((TM_MIX, dm), lambda i: (i, j))

    x2 = pl.pallas_call(
        functools.partial(_mixout_kernel, n_ctx_blocks=ncb_m, blocks_per_seq=ts // TM_MIX,
                          ctx_seg=tp),
        grid=(n_tok // TM_MIX,),
        in_specs=[pl.BlockSpec((TM_MIX, dm), lambda i: (jnp.minimum(i, ncb_m - 1), 0)),
                  pl.BlockSpec((TM_MIX, dm), lambda i: (jnp.maximum(i - ncb_m, 0), 0)),
                  zcol(ZO), zcol(ZB), zcol(ZC), zcol(ZX), zcol(ZGM), zcol(ZGS), tokm,
                  mod_spec, _resident((3, dm)), _resident((1, dm)),
                  _resident((dm, dm)), _resident((dm, dm)), _resident((dm, dm))],
        out_specs=tokm,
        out_shape=jax.ShapeDtypeStruct((n_tok, dm), F32),
        compiler_params=_params(1),
        name="mixout",
    )(hn_ctx, hn_lat, z, z, z, z, z, z, x1, mod, conv_w[0],
      conv_b[0][None, :], w_ml_out[0].astype(BF16), w_sc_out[0].astype(BF16), w_o[0].astype(BF16))

    y_p, y_s = pl.pallas_call(
        functools.partial(_ffn2_kernel, chunks=chunks, n_ctx_blocks=ncb,
                          blocks_per_seq=ts // TM_FFN),
        grid=(FFN_WSTEPS + nb,),
        in_specs=[tok_spec, mod_spec, _resident((3, dm)), _resident((1, dm))]
        + ffn_w_specs,
        out_specs=[pl.BlockSpec((TM_FFN, dm), lambda i: (jnp.minimum(tok(i), ncb - 1), 0)),
                   pl.BlockSpec((TM_FFN, dm), lambda i: (jnp.maximum(tok(i) - ncb, 0), 0))],
        out_shape=[jax.ShapeDtypeStruct((n_ctx, dm), F32), jax.ShapeDtypeStruct((n_lat, dm), F32)],
        scratch_shapes=ffn_w_scratch,
        compiler_params=_params(1),
        name="ffn2",
    )(x2, mod, norm_g[0], final_norm[None, :], ffn2_w1[0], ffn2_w2[0])

    return (y_p.reshape(bp, tp, dm), y_s.reshape(bs, ts, dm), new_c,
            new_n.reshape(bp, 1, N_DIR, NH, dh), new_m[..., 0, 0].reshape(bp, 1, N_DIR, NH))
```

```python
import functools

import jax
import jax.numpy as jnp
from jax import lax
from jax.experimental import pallas as pl
from jax.experimental.pallas import tpu as pltpu

F32 = jnp.float32
BF16 = jnp.bfloat16

NH = 4
N_DIR = 2
N_MOD = 9
GRID_W = 64
EPS = 1e-6
LANES = 128
SUBLANES = 8
VMEM_LIMIT = 56 * 1024 * 1024

TM_FFN = 512
FFN_SUB = 256
FFN_WSTEPS = 11
TM_MIX = 512
MIX_SUB = 256
TM_IN = 2048
TN_IN = 1024
TM_K = 1024
CHUNK = 256
CTX_SEQS = 8
ZQ, ZV, ZO, ZB, ZC, ZX, ZGM, ZGS = range(8)
SIG_GROUPS = (ZO, ZGM, ZGS)


def _params(n_axes):
    return pltpu.CompilerParams(dimension_semantics=("arbitrary",) * n_axes,
                                vmem_limit_bytes=VMEM_LIMIT)


def _resident(shape):
    zeros = (0,) * len(shape)
    return pl.BlockSpec(shape, lambda *_: zeros, pipeline_mode=pl.Buffered(1))


def _rms(x, g):
    return x * lax.rsqrt(jnp.mean(x * x, axis=-1, keepdims=True) + EPS) * g


def _bdot(a, b):
    return jnp.dot(a, b, preferred_element_type=F32)


def _mod_rows(mod_ref, t, n_ctx_blocks, blocks_per_seq):
    r = jnp.where(t < n_ctx_blocks, 0, 1 + (t - n_ctx_blocks) // blocks_per_seq)
    return [mod_ref[k, pl.ds(r, 1), :] for k in range(N_MOD)]


def _swiglu_residual(x, mod, g_norm, w1_ref, w2_ref, chunks, row0):
    d_ff = w2_ref.shape[0]
    h = (_rms(x, g_norm) * (1.0 + mod[row0 + 1]) + mod[row0]).astype(BF16)
    acc = None
    for lo, hi in chunks:
        a = _bdot(h, w1_ref[:, lo:hi])
        b = _bdot(h, w1_ref[:, d_ff + lo:d_ff + hi])
        part = _bdot((jax.nn.silu(a) * b).astype(BF16), w2_ref[lo:hi, :])
        acc = part if acc is None else acc + part
    return x + (0.5 * mod[row0 + 2]) * acc


def _row_parts(tm):
    return [pl.ds(r, FFN_SUB) for r in range(0, tm, FFN_SUB)]


def _stage_weights(i, w1f_ref, w2f_ref, w1_sc, w2_sc, also=None):
    c1, c2 = w1f_ref.shape[1], w2f_ref.shape[0]
    for c in range(FFN_WSTEPS):
        @pl.when(i == c)
        def _(c=c):
            w1_sc[:, c * c1:(c + 1) * c1] = w1f_ref[...].astype(BF16)
            w2_sc[c * c2:(c + 1) * c2, :] = w2f_ref[...].astype(BF16)
            if also is not None:
                also(c)


def _ffn1_kernel(xp_ref, xs_ref, condT_ref, wmod_ref, bmod_ref, g_ref, w1f_ref, w2f_ref,
                 x1_ref, h2_ref, mod_ref, w1_sc, w2_sc, s_sc, *,
                 chunks, n_ctx_blocks, blocks_per_seq, n_cond):
    i = pl.program_id(0)

    def mod_chunk(c):
        if c == 0:
            s = jax.nn.silu(condT_ref[...])
            for r in range(n_cond):
                s_sc[r] = jnp.broadcast_to(s[:, r:r + 1], s_sc.shape[1:])
        if c < N_MOD:
            w = wmod_ref[...]
            reps = w.shape[1] // LANES
            rows = [jnp.sum(w * jnp.concatenate([s_sc[r]] * reps, axis=1), axis=0, keepdims=True)
                    for r in range(n_cond)]
            rows.append(jnp.zeros((SUBLANES - n_cond, w.shape[1]), F32))
            mod_ref[c] = jnp.concatenate(rows, axis=0) + bmod_ref[c:c + 1, :]

    _stage_weights(i, w1f_ref, w2f_ref, w1_sc, w2_sc, also=mod_chunk)

    @pl.when(i >= FFN_WSTEPS)
    def _():
        t = i - FFN_WSTEPS
        is_ctx = t < n_ctx_blocks
        mod = _mod_rows(mod_ref, t, n_ctx_blocks, blocks_per_seq)
        for rows in _row_parts(x1_ref.shape[0]):
            x = jnp.where(is_ctx, xp_ref[rows, :], xs_ref[rows, :])
            x1 = _swiglu_residual(x, mod, g_ref[0:1], w1_sc, w2_sc, chunks, 0)
            x1_ref[rows, :] = x1
            h2_ref[rows, :] = (_rms(x1, g_ref[1:2]) * (1.0 + mod[4]) + mod[3]).astype(BF16)


def _ffn2_kernel(x_ref, mod_ref, g_ref, fin_ref, w1f_ref, w2f_ref, yp_ref, ys_ref, w1_sc, w2_sc, *,
                 chunks, n_ctx_blocks, blocks_per_seq):
    i = pl.program_id(0)
    _stage_weights(i, w1f_ref, w2f_ref, w1_sc, w2_sc)

    @pl.when(i >= FFN_WSTEPS)
    def _():
        mod = _mod_rows(mod_ref, i - FFN_WSTEPS, n_ctx_blocks, blocks_per_seq)
        y = jnp.concatenate(
            [_rms(_swiglu_residual(x_ref[rows, :], mod, g_ref[2:3], w1_sc, w2_sc, chunks, 6),
                  fin_ref[...]) for rows in _row_parts(x_ref.shape[0])], axis=0)

        @pl.when(i - FFN_WSTEPS < n_ctx_blocks)
        def _():
            yp_ref[...] = y

        @pl.when(i - FFN_WSTEPS >= n_ctx_blocks)
        def _():
            ys_ref[...] = y


def _ff_chunks(d_ff):
    tile = 256
    half = max(tile, (d_ff // 2 + tile - 1) // tile * tile)
    return ((0, half), (half, d_ff)) if half < d_ff else ((0, d_ff),)


def _inproj_kernel(h_ref, wa_ref, wb_ref, b_ref, z_ref, w_sc, *, n_lead, shift):
    j, i = pl.program_id(0), pl.program_id(1)

    @pl.when(jnp.logical_and(i == 0, j < n_lead))
    def _():
        w_sc[...] = wa_ref[...].astype(BF16)

    @pl.when(jnp.logical_and(i == 0, j >= n_lead))
    def _():
        w_sc[...] = jnp.concatenate([wa_ref[shift:, :], wb_ref[:shift, :]], axis=0).astype(BF16)

    def project():
        return lax.dot_general(h_ref[...], w_sc[...], (((1,), (1,)), ((), ())),
                               preferred_element_type=F32) + b_ref[...]

    is_sig = functools.reduce(jnp.logical_or, [j == g for g in SIG_GROUPS])

    @pl.when(is_sig)
    def _():
        z_ref[...] = (0.5 * jnp.tanh(0.5 * project()) + 0.5).astype(BF16)

    @pl.when(jnp.logical_not(is_sig))
    def _():
        z_ref[...] = project().astype(BF16)


def _kproj_kernel(h_ref, wt_ref, b_ref, kt_ref, gt_ref):
    dm = kt_ref.shape[1]
    bias = jnp.concatenate([b_ref[...]] * (h_ref.shape[0] // LANES), axis=1)
    full = lax.dot_general(wt_ref[...], h_ref[...], (((1,), (1,)), ((), ())),
                           preferred_element_type=F32) + bias
    for c in range(kt_ref.shape[0]):
        kt_ref[c] = full[:dm, c * CHUNK:(c + 1) * CHUNK].astype(BF16)
    gt_ref[...] = full[dm:, :]


def _lane_scan(x, op, fill, reverse):
    n = x.shape[1]
    lane = lax.broadcasted_iota(jnp.int32, x.shape, 1)
    sh = 1
    while sh < n:
        if reverse:
            nb = jnp.where(lane < n - sh, pltpu.roll(x, n - sh, axis=1), fill)
        else:
            nb = jnp.where(lane >= sh, pltpu.roll(x, sh, axis=1), fill)
        x = op(x, nb)
        sh *= 2
    return x


def _mlstm_kernel(*refs, nc, n_sub, has_init, write_state, unroll):
    it = iter(refs)
    q_ref, kt_ref, v_ref, gr_ref, mln_ref = (next(it) for _ in range(5))
    if has_init:
        c0_ref, n0_ref, m0_ref = next(it), next(it), next(it)
    hn_ref = next(it)
    if write_state:
        cout_ref, nout_ref, mout_ref = next(it), next(it), next(it)
    c_sc, u_sc, wi_sc, cl_sc, wk_sc, vr_sc, dc_sc, mi_sc, mf_sc = (next(it) for _ in range(9))
    if nc > 1:
        h_sc = next(it)

    def head_norm(h):
        mean_sq = _bdot((h * h).astype(BF16), jnp.full((h.shape[1],) * 2, 1.0 / h.shape[1], BF16))
        return (h * lax.rsqrt(mean_sq + EPS) * mln_ref[...]).astype(BF16)

    L = CHUNK
    dh = q_ref.shape[1]
    rows = u_sc.shape[1]
    blk, h_idx = pl.program_id(0), pl.program_id(1)
    row = lax.broadcasted_iota(jnp.int32, (L, L), 0)
    col = lax.broadcasted_iota(jnp.int32, (L, L), 1)
    visible = (col <= row, col >= row)
    no_inbound = (not has_init) and nc == 1
    ones_ext = jnp.ones((L, LANES), BF16)

    for d in range(N_DIR):
        rev = d == 1
        ig = gr_ref[0, d, 0, 0]
        lf = jax.nn.log_sigmoid(gr_ref[1, d, 0, 0])
        b = _lane_scan(lf, jnp.add, 0.0, rev)
        tot = jnp.broadcast_to(b[:, 0:1] if rev else b[:, L - 1:L], (rows, L))
        vrow = ig - b
        cm = _lane_scan(vrow, jnp.maximum, -jnp.inf, rev)
        g = tot - b + ig
        gmax = jnp.broadcast_to(jnp.max(g, axis=1, keepdims=True), (rows, L))
        mi_sc[d] = jnp.zeros((rows, L), F32)
        mf_sc[d] = jnp.zeros(mf_sc.shape[1:], F32)
        for jj in range(n_sub):
            if has_init:
                m = jnp.full((1, L), m0_ref[((blk * n_sub + jj) * N_DIR + d) * NH + h_idx], F32)
            else:
                m = jnp.zeros((1, L), F32)
            for c in (range(nc - 1, -1, -1) if rev else range(nc)):
                r = jj * nc + c
                mi_sc[d, r:r + 1, :] = m
                m = jnp.maximum(tot[r:r + 1] + m, gmax[r:r + 1])
            mf_sc[d, jj:jj + 1, :] = m
        m_in = mi_sc[d]
        m_out = jnp.maximum(tot + m_in, gmax)
        mm = jnp.maximum(m_in, cm)
        u_sc[d] = -mm
        wi_sc[d] = jnp.exp(m_in - mm)
        cl_sc[d] = jnp.exp(-(b + mm))
        wk_sc[d] = jnp.exp(g - m_out)
        vr_sc[d] = vrow
        dc_sc[d] = jnp.exp(tot + m_in - m_out)

    def chain(d, r, r0, s_raw):
        one = pl.ds(r, 1)
        qc = q_ref[pl.ds(r0, L), :]
        ktc = kt_ref[r]
        v_ext = jnp.concatenate([v_ref[pl.ds(r0, L), :], ones_ext], axis=1)
        if s_raw is None:
            s_raw = _bdot(qc, ktc)
        def per_query(row_ref, width):
            col = jnp.broadcast_to(row_ref[d, one, :], (LANES, L)).T
            return jnp.concatenate([col] * (width // LANES), axis=1)

        s = s_raw * jnp.exp(jnp.where(visible[d], per_query(u_sc, L) + vr_sc[d, one, :], -jnp.inf))
        acc = _bdot(s.astype(BF16), v_ext)
        if not no_inbound:
            acc = acc + per_query(wi_sc, dh + LANES) * _bdot(qc, c_sc[d].astype(BF16))
        rden = 1.0 / jnp.maximum(jnp.abs(acc[:, dh:]), per_query(cl_sc, LANES))
        h = acc[:, :dh] * jnp.concatenate([rden] * (dh // LANES), axis=1)
        kw = (ktc.astype(F32) * wk_sc[d, one, :]).astype(BF16)
        kv = _bdot(kw, v_ext)
        if no_inbound:
            c_sc[d] = kv
        else:
            c_sc[d] = dc_sc[d, one, :][:, 0:1] * c_sc[d] + kv
        return h

    def do_seq(j):
        base = j * (nc * L)
        for d in range(N_DIR):
            if has_init:
                n_rep = jnp.broadcast_to(n0_ref[j, 0, d, pl.ds(h_idx, 1), :], (LANES, dh)).T
                c_sc[d] = jnp.concatenate([c0_ref[j, 0, d, 0], n_rep], axis=1)
            elif not no_inbound:
                c_sc[d] = jnp.zeros(c_sc.shape[1:], F32)

        if nc == 1:
            r0 = pl.multiple_of(base, L)
            s_raw = _bdot(q_ref[pl.ds(r0, L), :], kt_ref[j])
            hn_ref[pl.ds(r0, L), :] = head_norm(chain(0, j, r0, s_raw) + chain(1, j, r0, s_raw))
        else:
            def step(i, first_visit):
                for d in range(N_DIR):
                    c = i if d == 0 else nc - 1 - i
                    r0 = pl.multiple_of(base + c * L, L)
                    h = chain(d, j * nc + c, r0, None)
                    if first_visit:
                        h_sc[pl.ds(r0, L), :] = h
                    else:
                        hn_ref[pl.ds(r0, L), :] = head_norm(h_sc[pl.ds(r0, L), :] + h)

            lax.fori_loop(0, nc // 2, lambda i, carry: (step(i, True), carry)[1], 0,
                          unroll=unroll)
            lax.fori_loop(nc // 2, nc, lambda i, carry: (step(i, False), carry)[1], 0,
                          unroll=unroll)

        if write_state:
            for d in range(N_DIR):
                cout_ref[j, 0, d, 0] = c_sc[d, :, :dh]
                nout_ref[j, d, 0] = c_sc[d, :, dh:].T[0:1, :]
                mout_ref[j, d, 0] = mf_sc[d, pl.ds(j, 1), :][:, :LANES]

    if n_sub == 1:
        do_seq(0)
    else:
        lax.fori_loop(0, n_sub, lambda j, carry: (do_seq(j), carry)[1], 0, unroll=unroll)


def _mlstm_call(z, kt, gates_rows, ml_norm, state, *, n_seq, t_len, n_sub, tok_block0, dh, unroll):
    nc = t_len // CHUNK
    rows = gates_rows.shape[-2]
    assert rows == n_sub * nc and rows % SUBLANES == 0 and (nc == 1 or nc % 2 == 0)
    has_init = state is not None
    write_state = not has_init
    kern = functools.partial(_mlstm_kernel, nc=nc, n_sub=n_sub, has_init=has_init,
                             write_state=write_state, unroll=unroll)
    tb = n_sub * t_len

    def z_spec(group):
        return pl.BlockSpec((tb, dh), lambda b, h: (tok_block0 + b, group * NH + h))

    in_specs = [z_spec(ZQ),
                pl.BlockSpec((rows, dh, CHUNK), lambda b, h: (tok_block0 + b, h, 0)),
                z_spec(ZV),
                pl.BlockSpec((2, N_DIR, 1, 1, rows, CHUNK), lambda b, h: (0, 0, h, b, 0, 0)),
                pl.BlockSpec((1, dh), lambda b, h: (0, h))]
    args = [z, kt, z, gates_rows, ml_norm]
    if has_init:
        c0, n0, m0 = state
        in_specs += [pl.BlockSpec((n_sub, 1, N_DIR, 1, dh, dh), lambda b, h: (b, 0, 0, h, 0, 0)),
                     pl.BlockSpec((n_sub, 1, N_DIR, NH, dh), lambda b, h: (b, 0, 0, 0, 0)),
                     pl.BlockSpec(memory_space=pltpu.SMEM)]
        args += [c0, n0, m0.reshape(-1)]
    out_specs = [pl.BlockSpec((tb, dh), lambda b, h: (b, h))]
    out_shape = [jax.ShapeDtypeStruct((n_seq * t_len, NH * dh), BF16)]
    if write_state:
        out_specs += [pl.BlockSpec((n_sub, 1, N_DIR, 1, dh, dh), lambda b, h: (b, 0, 0, h, 0, 0)),
                      pl.BlockSpec((n_sub, N_DIR, 1, 1, dh), lambda b, h: (b, 0, h, 0, 0)),
                      pl.BlockSpec((n_sub, N_DIR, 1, 1, LANES), lambda b, h: (b, 0, h, 0, 0))]
        out_shape += [jax.ShapeDtypeStruct((n_seq, 1, N_DIR, NH, dh, dh), F32),
                      jax.ShapeDtypeStruct((n_seq, N_DIR, NH, 1, dh), F32),
                      jax.ShapeDtypeStruct((n_seq, N_DIR, NH, 1, LANES), F32)]
    per_row = pltpu.VMEM((N_DIR, rows, CHUNK), F32)
    return pl.pallas_call(
        kern,
        grid=(n_seq // n_sub, NH),
        in_specs=in_specs,
        out_specs=out_specs,
        out_shape=out_shape,
        scratch_shapes=[pltpu.VMEM((N_DIR, dh, dh + LANES), F32)] + [per_row] * 7
        + [pltpu.VMEM((N_DIR, max(SUBLANES, n_sub), CHUNK), F32)]
        + ([pltpu.VMEM((tb, dh), F32)] if nc > 1 else []),
        compiler_params=_params(2),
        name="mlstm_lat" if has_init else "mlstm_ctx",
    )(*args)


def _mixout_kernel(hnc_ref, hnl_ref, zo_ref, zb_ref, zc_ref, zx_ref, zgm_ref, zgs_ref, x1_ref,
                   mod_ref, cw_ref, cb_ref, wml_ref, wsc_ref, wo_ref, x2_ref, *,
                   n_ctx_blocks, blocks_per_seq, ctx_seg):
    is_ctx = pl.program_id(0) < n_ctx_blocks
    gate = _mod_rows(mod_ref, pl.program_id(0), n_ctx_blocks, blocks_per_seq)[5]
    sub = MIX_SUB
    t = lax.broadcasted_iota(jnp.int32, (sub, 1), 0)
    pos = jnp.where(is_ctx, t % ctx_seg, t % GRID_W)
    has_prev = pos != 0
    has_next = pos != jnp.where(is_ctx, ctx_seg - 1, GRID_W - 1)
    for r in range(0, x1_ref.shape[0], sub):
        rows = pl.ds(r, sub)
        hn = jnp.where(is_ctx, hnc_ref[rows, :], hnl_ref[rows, :])
        ml = _bdot(zo_ref[rows, :] * hn, wml_ref[...])

        u = zc_ref[rows, :].astype(F32) * zx_ref[rows, :].astype(F32)
        u_prev = jnp.where(has_prev, pltpu.roll(u, 1, axis=0), 0.0)
        u_next = jnp.where(has_next, pltpu.roll(u, sub - 1, axis=0), 0.0)
        uc = u_prev * cw_ref[0:1] + u * cw_ref[1:2] + u_next * cw_ref[2:3] + cb_ref[...]
        sc = _bdot(zb_ref[rows, :] * uc.astype(BF16), wsc_ref[...])

        y = zgm_ref[rows, :].astype(F32) * ml + zgs_ref[rows, :].astype(F32) * sc
        x2_ref[rows, :] = x1_ref[rows, :] + gate * _bdot(y.astype(BF16), wo_ref[...])


def kernel(x_prompt, x_sample, state_C, state_n, state_m, c, c_ctx, w_mod, b_mod, norm_g,
           ffn1_w1, ffn1_w2, w_in, b_in, conv_w, conv_b, ml_norm, w_ml_out, w_sc_out, w_o,
           ffn2_w1, ffn2_w2, final_norm):
    assert w_mod.shape[0] == 1, "single trunk layer only"
    bp, tp, dm = x_prompt.shape
    bs, ts, _ = x_sample.shape
    n_ctx, n_lat = bp * tp, bs * ts
    n_tok = n_ctx + n_lat
    d_ff = ffn1_w2.shape[1]
    ml_dim = w_ml_out.shape[1]
    sc_dim = w_sc_out.shape[1]
    dh = ml_dim // NH
    n_gate = 2 * N_DIR * NH
    assert ml_dim == dm and sc_dim == dm and dh % 256 == 0 and dm == TN_IN
    assert w_in.shape[2] == 9 * dm + n_gate and n_gate < LANES
    assert n_ctx % TM_FFN == 0 and ts % TM_FFN == 0 and n_tok % TM_IN == 0
    assert n_ctx % TM_MIX == 0 and ts % TM_MIX == 0 and MIX_SUB % tp == 0 and MIX_SUB % GRID_W == 0
    assert TM_K % CHUNK == 0 and n_tok % TM_K == 0
    assert tp == CHUNK and ts % (2 * CHUNK) == 0 and bp % CTX_SEQS == 0 and n_ctx % ts == 0
    n_cond = 1 + bs

    cond = jnp.concatenate([c_ctx[None, :], c], axis=0)
    condT = jnp.pad(cond.T, ((0, 0), (0, SUBLANES - n_cond)))
    mod_spec = _resident((N_MOD, SUBLANES, dm))

    assert w_mod.shape[2] == N_MOD * dm and N_MOD <= FFN_WSTEPS
    w_in0, b_in0 = w_in[0], b_in[0]
    gate_lo = 4 * ml_dim
    ncb = n_ctx // TM_FFN
    nb = n_tok // TM_FFN
    chunks = _ff_chunks(d_ff)
    wc1, wc2 = 2 * d_ff // FFN_WSTEPS, d_ff // FFN_WSTEPS
    assert wc1 * FFN_WSTEPS == 2 * d_ff and wc1 % LANES == 0
    assert wc2 * FFN_WSTEPS == d_ff and wc2 % (2 * SUBLANES) == 0

    def tok(i):
        return jnp.maximum(i - FFN_WSTEPS, 0)

    def wstep(i):
        return jnp.minimum(i, FFN_WSTEPS - 1)

    ffn_w_specs = [pl.BlockSpec((dm, wc1), lambda i: (0, wstep(i))),
                   pl.BlockSpec((wc2, dm), lambda i: (wstep(i), 0))]
    ffn_w_scratch = [pltpu.VMEM((dm, 2 * d_ff), BF16), pltpu.VMEM((d_ff, dm), BF16)]
    tok_spec = pl.BlockSpec((TM_FFN, dm), lambda i: (tok(i), 0))
    x1, h2, mod = pl.pallas_call(
        functools.partial(_ffn1_kernel, chunks=chunks, n_ctx_blocks=ncb,
                          blocks_per_seq=ts // TM_FFN, n_cond=n_cond),
        grid=(FFN_WSTEPS + nb,),
        in_specs=[pl.BlockSpec((TM_FFN, dm), lambda i: (jnp.minimum(tok(i), ncb - 1), 0)),
                  pl.BlockSpec((TM_FFN, dm), lambda i: (jnp.maximum(tok(i) - ncb, 0), 0)),
                  _resident((dm, SUBLANES)),
                  pl.BlockSpec((dm, dm), lambda i: (0, jnp.minimum(i, N_MOD - 1))),
                  _resident((N_MOD, dm)), _resident((3, dm))] + ffn_w_specs,
        out_specs=[tok_spec, tok_spec,
                   pl.BlockSpec((N_MOD, SUBLANES, dm), lambda i: (0, 0, 0))],
        out_shape=[jax.ShapeDtypeStruct((n_tok, dm), F32), jax.ShapeDtypeStruct((n_tok, dm), BF16),
                   jax.ShapeDtypeStruct((N_MOD, SUBLANES, dm), F32)],
        scratch_shapes=ffn_w_scratch + [pltpu.VMEM((n_cond, dm, LANES), F32)],
        compiler_params=_params(1),
        name="ffn1",
    )(x_prompt.reshape(n_ctx, dm), x_sample.reshape(n_lat, dm), condT, w_mod[0],
      b_mod[0].reshape(N_MOD, dm), norm_g[0], ffn1_w1[0], ffn1_w2[0])

    w_in_t = w_in0.T
    n_lead = gate_lo // TN_IN - 1
    b_main = jnp.concatenate([b_in0[:ml_dim], b_in0[2 * ml_dim:gate_lo],
                              b_in0[gate_lo + n_gate:]])[None, :]
    n_main = b_main.shape[1]

    def w_block(j):
        return jnp.where(j == 0, 0, j + 1)

    z = pl.pallas_call(
        functools.partial(_inproj_kernel, n_lead=n_lead, shift=n_gate),
        grid=(n_main // TN_IN, n_tok // TM_IN),
        in_specs=[pl.BlockSpec((TM_IN, dm), lambda j, i: (i, 0)),
                  pl.BlockSpec((TN_IN, dm), lambda j, i: (w_block(j), 0)),
                  pl.BlockSpec((TN_IN, dm), lambda j, i: (w_block(j) + 1, 0)),
                  pl.BlockSpec((1, TN_IN), lambda j, i: (0, j))],
        out_specs=pl.BlockSpec((TM_IN, TN_IN), lambda j, i: (i, j)),
        out_shape=jax.ShapeDtypeStruct((n_tok, n_main), BF16),
        scratch_shapes=[pltpu.VMEM((TN_IN, dm), BF16)],
        compiler_params=_params(2),
        name="inproj",
    )(h2, w_in_t, w_in_t, b_main)

    kscale = dh ** -0.5
    w_kg = jnp.concatenate([w_in_t[ml_dim:2 * ml_dim] * kscale,
                            w_in_t[gate_lo:gate_lo + n_gate]], axis=0).astype(BF16)
    b_kg = jnp.concatenate([b_in0[ml_dim:2 * ml_dim] * kscale,
                            b_in0[gate_lo:gate_lo + n_gate]])
    b_kg = jnp.broadcast_to(b_kg[:, None], (dm + n_gate, LANES))
    kt, gates_t = pl.pallas_call(
        _kproj_kernel,
        grid=(n_tok // TM_K,),
        in_specs=[pl.BlockSpec((TM_K, dm), lambda i: (i, 0)), _resident((dm + n_gate, dm)),
                  _resident((dm + n_gate, LANES))],
        out_specs=[pl.BlockSpec((TM_K // CHUNK, dm, CHUNK), lambda i: (i, 0, 0)),
                   pl.BlockSpec((n_gate, TM_K), lambda i: (0, i))],
        out_shape=[jax.ShapeDtypeStruct((n_tok // CHUNK, dm, CHUNK), BF16),
                   jax.ShapeDtypeStruct((n_gate, n_tok), F32)],
        compiler_params=_params(1),
        name="kproj",
    )(h2, w_kg, b_kg)

    g_rows = gates_t.reshape(2, N_DIR, NH, n_tok)
    rows_ctx = CTX_SEQS * (tp // CHUNK)
    hn_ctx, new_c, new_n, new_m = _mlstm_call(
        z, kt, g_rows[..., :n_ctx].reshape(2, N_DIR, NH, bp // CTX_SEQS, rows_ctx, CHUNK),
        ml_norm, None,
        n_seq=bp, t_len=tp, n_sub=CTX_SEQS, tok_block0=0, dh=dh, unroll=4)
    (hn_lat,) = _mlstm_call(
        z, kt, g_rows[..., n_ctx:].reshape(2, N_DIR, NH, bs, ts // CHUNK, CHUNK),
        ml_norm, (state_C, state_n, state_m),
        n_seq=bs, t_len=ts, n_sub=1, tok_block0=n_ctx // ts, dh=dh, unroll=4)

    ncb_m = n_ctx // TM_MIX
    tokm = pl.BlockSpec((TM_MIX, dm), lambda i: (i, 0))

    def zcol(j):
        return pl.BlockSpec((TM_MIX, dm), lambda i: (i, j))

    x2 = pl.pallas_call(
        functools.partial(_mixout_kernel, n_ctx_blocks=ncb_m, blocks_per_seq=ts // TM_MIX,
                          ctx_seg=tp),
        grid=(n_tok // TM_MIX,),
        in_specs=[pl.BlockSpec((TM_MIX, dm), lambda i: (jnp.minimum(i, ncb_m - 1), 0)),
                  pl.BlockSpec((TM_MIX, dm), lambda i: (jnp.maximum(i - ncb_m, 0), 0)),
                  zcol(ZO), zcol(ZB), zcol(ZC), zcol(ZX), zcol(ZGM), zcol(ZGS), tokm,
                  mod_spec, _resident((3, dm)), _resident((1, dm)),
                  _resident((dm, dm)), _resident((dm, dm)), _resident((dm, dm))],
        out_specs=tokm,
        out_shape=jax.ShapeDtypeStruct((n_tok, dm), F32),
        compiler_params=_params(1),
        name="mixout",
    )(hn_ctx, hn_lat, z, z, z, z, z, z, x1, mod, conv_w[0],
      conv_b[0][None, :], w_ml_out[0].astype(BF16), w_sc_out[0].astype(BF16), w_o[0].astype(BF16))

    y_p, y_s = pl.pallas_call(
        functools.partial(_ffn2_kernel, chunks=chunks, n_ctx_blocks=ncb,
                          blocks_per_seq=ts // TM_FFN),
        grid=(FFN_WSTEPS + nb,),
        in_specs=[tok_spec, mod_spec, _resident((3, dm)), _resident((1, dm))]
        + ffn_w_specs,
        out_specs=[pl.BlockSpec((TM_FFN, dm), lambda i: (jnp.minimum(tok(i), ncb - 1), 0)),
                   pl.BlockSpec((TM_FFN, dm), lambda i: (jnp.maximum(tok(i) - ncb, 0), 0))],
        out_shape=[jax.ShapeDtypeStruct((n_ctx, dm), F32), jax.ShapeDtypeStruct((n_lat, dm), F32)],
        scratch_shapes=ffn_w_scratch,
        compiler_params=_params(1),
        name="ffn2",
    )(x2, mod, norm_g[0], final_norm[None, :], ffn2_w1[0], ffn2_w2[0])

    return (y_p.reshape(bp, tp, dm), y_s.reshape(bs, ts, dm), new_c,
            new_n.reshape(bp, 1, N_DIR, NH, dh), new_m[..., 0, 0].reshape(bp, 1, N_DIR, NH))
```

```python
import functools

import jax
import jax.numpy as jnp
from jax import lax
from jax.experimental import pallas as pl
from jax.experimental.pallas import tpu as pltpu

F32 = jnp.float32
BF16 = jnp.bfloat16

NH = 4
N_DIR = 2
N_MOD = 9
GRID_W = 64
EPS = 1e-6
LANES = 128
SUBLANES = 8
MXU_TILE = 256
VMEM_LIMIT = 56 * 1024 * 1024

TM_FFN = 512
FFN_SUB = 256
FFN_WSTEPS = 11
TM_MIX = 512
MIX_SUB = 256
TM_IN = 2048
TN_IN = 1024
TM_K = 1024
CHUNK = 256
CTX_SEQS = 8
ZQ, ZV, ZO, ZB, ZC, ZX, ZGM, ZGS = range(8)
SIG_GROUPS = (ZO, ZGM, ZGS)


def _params(n_axes):
    return pltpu.CompilerParams(dimension_semantics=("arbitrary",) * n_axes,
                                vmem_limit_bytes=VMEM_LIMIT)


def _resident(shape):
    zeros = (0,) * len(shape)
    return pl.BlockSpec(shape, lambda *_: zeros, pipeline_mode=pl.Buffered(1))


def _rms(x, g):
    return x * lax.rsqrt(jnp.mean(x * x, axis=-1, keepdims=True) + EPS) * g


def _bdot(a, b):
    return jnp.dot(a, b, preferred_element_type=F32)


def _mod_rows(mod_ref, t, n_ctx_blocks, blocks_per_seq):
    r = jnp.where(t < n_ctx_blocks, 0, 1 + (t - n_ctx_blocks) // blocks_per_seq)
    return [mod_ref[k, pl.ds(r, 1), :] for k in range(N_MOD)]


def _swiglu_residual(x, mod, g_norm, w1_ref, w2_ref, chunks, row0):
    d_ff = w2_ref.shape[0]
    h = (_rms(x, g_norm) * (1.0 + mod[row0 + 1]) + mod[row0]).astype(BF16)
    acc = None
    for lo, hi in chunks:
        a = _bdot(h, w1_ref[:, lo:hi])
        b = _bdot(h, w1_ref[:, d_ff + lo:d_ff + hi])
        part = _bdot((jax.nn.silu(a) * b).astype(BF16), w2_ref[lo:hi, :])
        acc = part if acc is None else acc + part
    return x + (0.5 * mod[row0 + 2]) * acc


def _row_parts(tm):
    return [pl.ds(r, FFN_SUB) for r in range(0, tm, FFN_SUB)]


def _stage_weights(i, w1f_ref, w2f_ref, w1_sc, w2_sc, also=None):
    c1, c2 = w1f_ref.shape[1], w2f_ref.shape[0]
    for c in range(FFN_WSTEPS):
        @pl.when(i == c)
        def _(c=c):
            w1_sc[:, c * c1:(c + 1) * c1] = w1f_ref[...].astype(BF16)
            w2_sc[c * c2:(c + 1) * c2, :] = w2f_ref[...].astype(BF16)
            if also is not None:
                also(c)


def _ffn1_kernel(xp_ref, xs_ref, condT_ref, wmod_ref, bmod_ref, g_ref, w1f_ref, w2f_ref,
                 x1_ref, h2_ref, mod_ref, w1_sc, w2_sc, *,
                 chunks, n_ctx_blocks, blocks_per_seq, n_cond):
    i = pl.program_id(0)

    def mod_chunk(c):
        if c < N_MOD:
            s = jax.nn.silu(condT_ref[...])
            w = wmod_ref[...]
            rows = [jnp.sum(w * s[:, r:r + 1], axis=0, keepdims=True) for r in range(n_cond)]
            rows.append(jnp.zeros((SUBLANES - n_cond, w.shape[1]), F32))
            mod_ref[c] = jnp.concatenate(rows, axis=0) + bmod_ref[c:c + 1, :]

    _stage_weights(i, w1f_ref, w2f_ref, w1_sc, w2_sc, also=mod_chunk)

    @pl.when(i >= FFN_WSTEPS)
    def _():
        t = i - FFN_WSTEPS
        is_ctx = t < n_ctx_blocks
        mod = _mod_rows(mod_ref, t, n_ctx_blocks, blocks_per_seq)
        for rows in _row_parts(x1_ref.shape[0]):
            x = jnp.where(is_ctx, xp_ref[rows, :], xs_ref[rows, :])
            x1 = _swiglu_residual(x, mod, g_ref[0:1], w1_sc, w2_sc, chunks, 0)
            x1_ref[rows, :] = x1
            h2_ref[rows, :] = (_rms(x1, g_ref[1:2]) * (1.0 + mod[4]) + mod[3]).astype(BF16)


def _ffn2_kernel(x_ref, mod_ref, g_ref, fin_ref, w1f_ref, w2f_ref, yp_ref, ys_ref, w1_sc, w2_sc, *,
                 chunks, n_ctx_blocks, blocks_per_seq):
    i = pl.program_id(0)
    _stage_weights(i, w1f_ref, w2f_ref, w1_sc, w2_sc)

    @pl.when(i >= FFN_WSTEPS)
    def _():
        mod = _mod_rows(mod_ref, i - FFN_WSTEPS, n_ctx_blocks, blocks_per_seq)
        y = jnp.concatenate(
            [_rms(_swiglu_residual(x_ref[rows, :], mod, g_ref[2:3], w1_sc, w2_sc, chunks, 6),
                  fin_ref[...]) for rows in _row_parts(x_ref.shape[0])], axis=0)

        @pl.when(i - FFN_WSTEPS < n_ctx_blocks)
        def _():
            yp_ref[...] = y

        @pl.when(i - FFN_WSTEPS >= n_ctx_blocks)
        def _():
            ys_ref[...] = y


def _ff_chunks(d_ff):
    tile = MXU_TILE
    half = max(tile, (d_ff // 2 + tile - 1) // tile * tile)
    return ((0, half), (half, d_ff)) if half < d_ff else ((0, d_ff),)


def _inproj_kernel(h_ref, wa_ref, wb_ref, b_ref, z_ref, w_sc, *, n_lead, shift):
    j, i = pl.program_id(0), pl.program_id(1)

    @pl.when(jnp.logical_and(i == 0, j < n_lead))
    def _():
        w_sc[...] = wa_ref[...].astype(BF16)

    @pl.when(jnp.logical_and(i == 0, j >= n_lead))
    def _():
        w_sc[...] = jnp.concatenate([wa_ref[shift:, :], wb_ref[:shift, :]], axis=0).astype(BF16)

    def project():
        return lax.dot_general(h_ref[...], w_sc[...], (((1,), (1,)), ((), ())),
                               preferred_element_type=F32) + b_ref[...]

    is_sig = functools.reduce(jnp.logical_or, [j == g for g in SIG_GROUPS])

    @pl.when(is_sig)
    def _():
        z_ref[...] = (0.5 * jnp.tanh(0.5 * project()) + 0.5).astype(BF16)

    @pl.when(jnp.logical_not(is_sig))
    def _():
        z_ref[...] = project().astype(BF16)


def _kproj_kernel(h_ref, wt_ref, b_ref, kt_ref, gt_ref):
    dm = kt_ref.shape[1]
    full = lax.dot_general(wt_ref[...], h_ref[...], (((1,), (1,)), ((), ())),
                           preferred_element_type=F32) + b_ref[...]
    for c in range(kt_ref.shape[0]):
        kt_ref[c] = full[:dm, c * CHUNK:(c + 1) * CHUNK].astype(BF16)
    gt_ref[...] = full[dm:, :]


def _lane_scan(x, op, fill, reverse):
    n = x.shape[1]
    lane = lax.broadcasted_iota(jnp.int32, x.shape, 1)
    sh = 1
    while sh < n:
        if reverse:
            nb = jnp.where(lane < n - sh, pltpu.roll(x, n - sh, axis=1), fill)
        else:
            nb = jnp.where(lane >= sh, pltpu.roll(x, sh, axis=1), fill)
        x = op(x, nb)
        sh *= 2
    return x


def _mlstm_kernel(*refs, nc, n_sub, has_init, write_state, unroll):
    it = iter(refs)
    q_ref, kt_ref, v_ref, gr_ref, mln_ref = (next(it) for _ in range(5))
    if has_init:
        c0_ref, n0_ref, m0_ref = next(it), next(it), next(it)
    hn_ref = next(it)
    if write_state:
        cout_ref, nout_ref, mout_ref = next(it), next(it), next(it)
    c_sc, u_sc, wi_sc, cl_sc, wk_sc, vr_sc, dc_sc, mi_sc, mf_sc = (next(it) for _ in range(9))
    if nc > 1:
        h_sc = next(it)

    def head_norm(h):
        mean_sq = _bdot((h * h).astype(BF16), jnp.full((h.shape[1],) * 2, 1.0 / h.shape[1], BF16))
        return (h * lax.rsqrt(mean_sq + EPS) * mln_ref[...]).astype(BF16)

    L = CHUNK
    dh = q_ref.shape[1]
    rows = u_sc.shape[1]
    blk, h_idx = pl.program_id(0), pl.program_id(1)
    row = lax.broadcasted_iota(jnp.int32, (L, L), 0)
    col = lax.broadcasted_iota(jnp.int32, (L, L), 1)
    visible = (col <= row, col >= row)
    no_inbound = (not has_init) and nc == 1
    ones_ext = jnp.ones((L, LANES), BF16)

    def gate_rows(gate, d):
        tokens = gr_ref[pl.ds((gate * N_DIR + d) * NH + h_idx, 1), :]
        return jnp.concatenate([tokens[:, r * L:(r + 1) * L] for r in range(rows)], axis=0)

    for d in range(N_DIR):
        rev = d == 1
        ig = gate_rows(0, d)
        lf = jax.nn.log_sigmoid(gate_rows(1, d))
        b = _lane_scan(lf, jnp.add, 0.0, rev)
        tot = jnp.broadcast_to(b[:, 0:1] if rev else b[:, L - 1:L], (rows, L))
        vrow = ig - b
        cm = _lane_scan(vrow, jnp.maximum, -jnp.inf, rev)
        g = tot - b + ig
        gmax = jnp.broadcast_to(jnp.max(g, axis=1, keepdims=True), (rows, L))
        mi_sc[d] = jnp.zeros((rows, L), F32)
        mf_sc[d] = jnp.zeros(mf_sc.shape[1:], F32)
        for jj in range(n_sub):
            if has_init:
                m = jnp.full((1, L), m0_ref[((blk * n_sub + jj) * N_DIR + d) * NH + h_idx], F32)
            else:
                m = jnp.zeros((1, L), F32)
            for c in (range(nc - 1, -1, -1) if rev else range(nc)):
                r = jj * nc + c
                mi_sc[d, r:r + 1, :] = m
                m = jnp.maximum(tot[r:r + 1] + m, gmax[r:r + 1])
            mf_sc[d, jj:jj + 1, :] = m
        m_in = mi_sc[d]
        m_out = jnp.maximum(tot + m_in, gmax)
        mm = jnp.maximum(m_in, cm)
        u_sc[d] = -mm
        wi_sc[d] = jnp.exp(m_in - mm)
        cl_sc[d] = jnp.exp(-(b + mm))
        wk_sc[d] = jnp.exp(g - m_out)
        vr_sc[d] = vrow
        dc_sc[d] = jnp.exp(tot + m_in - m_out)

    def chain(d, r, r0, s_raw):
        one = pl.ds(r, 1)
        qc = q_ref[pl.ds(r0, L), :]
        ktc = kt_ref[r]
        v_ext = jnp.concatenate([v_ref[pl.ds(r0, L), :], ones_ext], axis=1)
        if s_raw is None:
            s_raw = _bdot(qc, ktc)
        def per_query(row_ref, width):
            col = jnp.broadcast_to(row_ref[d, one, :], (LANES, L)).T
            return jnp.concatenate([col] * (width // LANES), axis=1)

        s = s_raw * jnp.exp(jnp.where(visible[d], per_query(u_sc, L) + vr_sc[d, one, :], -jnp.inf))
        acc = _bdot(s.astype(BF16), v_ext)
        if not no_inbound:
            acc = acc + per_query(wi_sc, dh + LANES) * _bdot(qc, c_sc[d].astype(BF16))
        rden = 1.0 / jnp.maximum(jnp.abs(acc[:, dh:]), per_query(cl_sc, LANES))
        h = acc[:, :dh] * jnp.concatenate([rden] * (dh // LANES), axis=1)
        kw = (ktc.astype(F32) * wk_sc[d, one, :]).astype(BF16)
        kv = _bdot(kw, v_ext)
        if no_inbound:
            c_sc[d] = kv
        else:
            c_sc[d] = dc_sc[d, one, :][:, 0:1] * c_sc[d] + kv
        return h

    def do_seq(j):
        base = j * (nc * L)
        for d in range(N_DIR):
            if has_init:
                n_rep = jnp.broadcast_to(n0_ref[j, 0, d, pl.ds(h_idx, 1), :], (LANES, dh)).T
                c_sc[d] = jnp.concatenate([c0_ref[j, 0, d, 0], n_rep], axis=1)
            elif not no_inbound:
                c_sc[d] = jnp.zeros(c_sc.shape[1:], F32)

        if nc == 1:
            r0 = pl.multiple_of(base, L)
            s_raw = _bdot(q_ref[pl.ds(r0, L), :], kt_ref[j])
            hn_ref[pl.ds(r0, L), :] = head_norm(chain(0, j, r0, s_raw) + chain(1, j, r0, s_raw))
        else:
            def step(i, first_visit):
                for d in range(N_DIR):
                    c = i if d == 0 else nc - 1 - i
                    r0 = pl.multiple_of(base + c * L, L)
                    h = chain(d, j * nc + c, r0, None)
                    if first_visit:
                        h_sc[pl.ds(r0, L), :] = h
                    else:
                        hn_ref[pl.ds(r0, L), :] = head_norm(h_sc[pl.ds(r0, L), :] + h)

            lax.fori_loop(0, nc // 2, lambda i, carry: (step(i, True), carry)[1], 0,
                          unroll=unroll)
            lax.fori_loop(nc // 2, nc, lambda i, carry: (step(i, False), carry)[1], 0,
                          unroll=unroll)

        if write_state:
            for d in range(N_DIR):
                cout_ref[j, 0, d, 0] = c_sc[d, :, :dh]
                nout_ref[j, 0, d, pl.ds(h_idx, 1), :] = c_sc[d, :, dh:].T[0:1, :]
                mout_ref[j, d, 0] = mf_sc[d, pl.ds(j, 1), :][:, :LANES]

    if n_sub == 1:
        do_seq(0)
    else:
        lax.fori_loop(0, n_sub, lambda j, carry: (do_seq(j), carry)[1], 0, unroll=unroll)


def _mlstm_call(z, kt, gates_t, ml_norm, state, *, n_seq, t_len, n_sub, tok_block0, dh, unroll):
    nc = t_len // CHUNK
    rows = n_sub * nc
    assert rows % SUBLANES == 0 and (nc == 1 or nc % 2 == 0)
    has_init = state is not None
    write_state = not has_init
    kern = functools.partial(_mlstm_kernel, nc=nc, n_sub=n_sub, has_init=has_init,
                             write_state=write_state, unroll=unroll)
    tb = n_sub * t_len

    def z_spec(group):
        return pl.BlockSpec((tb, dh), lambda b, h: (tok_block0 + b, group * NH + h))

    in_specs = [z_spec(ZQ),
                pl.BlockSpec((rows, dh, CHUNK), lambda b, h: (tok_block0 + b, h, 0)),
                z_spec(ZV),
                pl.BlockSpec((gates_t.shape[0], tb), lambda b, h: (0, tok_block0 + b)),
                pl.BlockSpec((1, dh), lambda b, h: (0, h))]
    args = [z, kt, z, gates_t, ml_norm]
    if has_init:
        c0, n0, m0 = state
        in_specs += [pl.BlockSpec((n_sub, 1, N_DIR, 1, dh, dh), lambda b, h: (b, 0, 0, h, 0, 0)),
                     pl.BlockSpec((n_sub, 1, N_DIR, NH, dh), lambda b, h: (b, 0, 0, 0, 0)),
                     pl.BlockSpec(memory_space=pltpu.SMEM)]
        args += [c0, n0, m0.reshape(-1)]
    out_specs = [pl.BlockSpec((tb, dh), lambda b, h: (b, h))]
    out_shape = [jax.ShapeDtypeStruct((n_seq * t_len, NH * dh), BF16)]
    if write_state:
        out_specs += [pl.BlockSpec((n_sub, 1, N_DIR, 1, dh, dh), lambda b, h: (b, 0, 0, h, 0, 0)),
                      pl.BlockSpec((n_sub, 1, N_DIR, NH, dh), lambda b, h: (b, 0, 0, 0, 0)),
                      pl.BlockSpec((n_sub, N_DIR, 1, 1, LANES), lambda b, h: (b, 0, h, 0, 0))]
        out_shape += [jax.ShapeDtypeStruct((n_seq, 1, N_DIR, NH, dh, dh), F32),
                      jax.ShapeDtypeStruct((n_seq, 1, N_DIR, NH, dh), F32),
                      jax.ShapeDtypeStruct((n_seq, N_DIR, NH, 1, LANES), F32)]
    per_row = pltpu.VMEM((N_DIR, rows, CHUNK), F32)
    return pl.pallas_call(
        kern,
        grid=(n_seq // n_sub, NH),
        in_specs=in_specs,
        out_specs=out_specs,
        out_shape=out_shape,
        scratch_shapes=[pltpu.VMEM((N_DIR, dh, dh + LANES), F32)] + [per_row] * 7
        + [pltpu.VMEM((N_DIR, max(SUBLANES, n_sub), CHUNK), F32)]
        + ([pltpu.VMEM((tb, dh), F32)] if nc > 1 else []),
        compiler_params=_params(2),
        name="mlstm_lat" if has_init else "mlstm_ctx",
    )(*args)


def _mixout_kernel(hnc_ref, hnl_ref, zo_ref, zb_ref, zc_ref, zx_ref, zgm_ref, zgs_ref, x1_ref,
                   mod_ref, cw_ref, cb_ref, wml_ref, wsc_ref, wo_ref, x2_ref, *,
                   n_ctx_blocks, blocks_per_seq, ctx_seg):
    is_ctx = pl.program_id(0) < n_ctx_blocks
    gate = _mod_rows(mod_ref, pl.program_id(0), n_ctx_blocks, blocks_per_seq)[5]
    sub = MIX_SUB
    t = lax.broadcasted_iota(jnp.int32, (sub, 1), 0)
    pos = jnp.where(is_ctx, t % ctx_seg, t % GRID_W)
    has_prev = pos != 0
    has_next = pos != jnp.where(is_ctx, ctx_seg - 1, GRID_W - 1)
    for r in range(0, x1_ref.shape[0], sub):
        rows = pl.ds(r, sub)
        hn = jnp.where(is_ctx, hnc_ref[rows, :], hnl_ref[rows, :])
        ml = _bdot(zo_ref[rows, :] * hn, wml_ref[...])

        u = zc_ref[rows, :].astype(F32) * zx_ref[rows, :].astype(F32)
        u_prev = jnp.where(has_prev, pltpu.roll(u, 1, axis=0), 0.0)
        u_next = jnp.where(has_next, pltpu.roll(u, sub - 1, axis=0), 0.0)
        uc = u_prev * cw_ref[0:1] + u * cw_ref[1:2] + u_next * cw_ref[2:3] + cb_ref[...]
        sc = _bdot(zb_ref[rows, :] * uc.astype(BF16), wsc_ref[...])

        y = zgm_ref[rows, :].astype(F32) * ml + zgs_ref[rows, :].astype(F32) * sc
        x2_ref[rows, :] = x1_ref[rows, :] + gate * _bdot(y.astype(BF16), wo_ref[...])


def kernel(x_prompt, x_sample, state_C, state_n, state_m, c, c_ctx, w_mod, b_mod, norm_g,
           ffn1_w1, ffn1_w2, w_in, b_in, conv_w, conv_b, ml_norm, w_ml_out, w_sc_out, w_o,
           ffn2_w1, ffn2_w2, final_norm):
    assert w_mod.shape[0] == 1, "single trunk layer only"
    bp, tp, dm = x_prompt.shape
    bs, ts, _ = x_sample.shape
    n_ctx, n_lat = bp * tp, bs * ts
    n_tok = n_ctx + n_lat
    d_ff = ffn1_w2.shape[1]
    ml_dim = w_ml_out.shape[1]
    sc_dim = w_sc_out.shape[1]
    dh = ml_dim // NH
    n_gate = 2 * N_DIR * NH
    assert ml_dim == dm and sc_dim == dm and dh % MXU_TILE == 0 and dm == TN_IN
    assert w_in.shape[2] == 9 * dm + n_gate and n_gate < LANES
    assert n_ctx % TM_FFN == 0 and ts % TM_FFN == 0 and n_tok % TM_IN == 0
    assert n_ctx % TM_MIX == 0 and ts % TM_MIX == 0 and MIX_SUB % tp == 0 and MIX_SUB % GRID_W == 0
    assert TM_K % CHUNK == 0 and n_tok % TM_K == 0
    assert tp == CHUNK and ts % (2 * CHUNK) == 0 and bp % CTX_SEQS == 0 and n_ctx % ts == 0
    n_cond = 1 + bs

    cond = jnp.concatenate([c_ctx[None, :], c], axis=0)
    condT = jnp.pad(cond.T, ((0, 0), (0, SUBLANES - n_cond)))
    mod_spec = _resident((N_MOD, SUBLANES, dm))

    assert w_mod.shape[2] == N_MOD * dm and N_MOD <= FFN_WSTEPS
    w_in0, b_in0 = w_in[0], b_in.reshape(-1)
    gate_lo = 4 * ml_dim
    ncb = n_ctx // TM_FFN
    nb = n_tok // TM_FFN
    chunks = _ff_chunks(d_ff)
    wc1, wc2 = 2 * d_ff // FFN_WSTEPS, d_ff // FFN_WSTEPS
    assert wc1 * FFN_WSTEPS == 2 * d_ff and wc1 % LANES == 0
    assert wc2 * FFN_WSTEPS == d_ff and wc2 % (2 * SUBLANES) == 0

    def tok(i):
        return jnp.maximum(i - FFN_WSTEPS, 0)

    def wstep(i):
        return jnp.minimum(i, FFN_WSTEPS - 1)

    ffn_w_specs = [pl.BlockSpec((dm, wc1), lambda i: (0, wstep(i))),
                   pl.BlockSpec((wc2, dm), lambda i: (wstep(i), 0))]
    ffn_w_scratch = [pltpu.VMEM((dm, 2 * d_ff), BF16), pltpu.VMEM((d_ff, dm), BF16)]
    tok_spec = pl.BlockSpec((TM_FFN, dm), lambda i: (tok(i), 0))
    x1, h2, mod = pl.pallas_call(
        functools.partial(_ffn1_kernel, chunks=chunks, n_ctx_blocks=ncb,
                          blocks_per_seq=ts // TM_FFN, n_cond=n_cond),
        grid=(FFN_WSTEPS + nb,),
        in_specs=[pl.BlockSpec((TM_FFN, dm), lambda i: (jnp.minimum(tok(i), ncb - 1), 0)),
                  pl.BlockSpec((TM_FFN, dm), lambda i: (jnp.maximum(tok(i) - ncb, 0), 0)),
                  _resident((dm, SUBLANES)),
                  pl.BlockSpec((dm, dm), lambda i: (0, jnp.minimum(i, N_MOD - 1))),
                  _resident((N_MOD, dm)), _resident((3, dm))] + ffn_w_specs,
        out_specs=[tok_spec, tok_spec,
                   pl.BlockSpec((N_MOD, SUBLANES, dm), lambda i: (0, 0, 0))],
        out_shape=[jax.ShapeDtypeStruct((n_tok, dm), F32), jax.ShapeDtypeStruct((n_tok, dm), BF16),
                   jax.ShapeDtypeStruct((N_MOD, SUBLANES, dm), F32)],
        scratch_shapes=ffn_w_scratch,
        compiler_params=_params(1),
        name="ffn1",
    )(x_prompt.reshape(n_ctx, dm), x_sample.reshape(n_lat, dm), condT, w_mod[0],
      b_mod.reshape(N_MOD, dm), norm_g[0], ffn1_w1[0], ffn1_w2[0])

    w_in_t = w_in0.T
    n_lead = gate_lo // TN_IN - 1
    b_main = jnp.concatenate([b_in0[:ml_dim], b_in0[2 * ml_dim:gate_lo],
                              b_in0[gate_lo + n_gate:]])[None, :]
    n_main = b_main.shape[1]

    def w_block(j):
        return jnp.where(j == 0, 0, j + 1)

    z = pl.pallas_call(
        functools.partial(_inproj_kernel, n_lead=n_lead, shift=n_gate),
        grid=(n_main // TN_IN, n_tok // TM_IN),
        in_specs=[pl.BlockSpec((TM_IN, dm), lambda j, i: (i, 0)),
                  pl.BlockSpec((TN_IN, dm), lambda j, i: (w_block(j), 0)),
                  pl.BlockSpec((TN_IN, dm), lambda j, i: (w_block(j) + 1, 0)),
                  pl.BlockSpec((1, TN_IN), lambda j, i: (0, j))],
        out_specs=pl.BlockSpec((TM_IN, TN_IN), lambda j, i: (i, j)),
        out_shape=jax.ShapeDtypeStruct((n_tok, n_main), BF16),
        scratch_shapes=[pltpu.VMEM((TN_IN, dm), BF16)],
        compiler_params=_params(2),
        name="inproj",
    )(h2, w_in_t, w_in_t, b_main)

    kscale = dh ** -0.5
    w_kg = jnp.concatenate([w_in_t[ml_dim:2 * ml_dim] * kscale,
                            w_in_t[gate_lo:gate_lo + n_gate]], axis=0).astype(BF16)
    b_kg = jnp.concatenate([b_in0[ml_dim:2 * ml_dim] * kscale,
                            b_in0[gate_lo:gate_lo + n_gate]])[:, None]
    kt, gates_t = pl.pallas_call(
        _kproj_kernel,
        grid=(n_tok // TM_K,),
        in_specs=[pl.BlockSpec((TM_K, dm), lambda i: (i, 0)), _resident((dm + n_gate, dm)),
                  _resident((dm + n_gate, 1))],
        out_specs=[pl.BlockSpec((TM_K // CHUNK, dm, CHUNK), lambda i: (i, 0, 0)),
                   pl.BlockSpec((n_gate, TM_K), lambda i: (0, i))],
        out_shape=[jax.ShapeDtypeStruct((n_tok // CHUNK, dm, CHUNK), BF16),
                   jax.ShapeDtypeStruct((n_gate, n_tok), F32)],
        compiler_params=_params(1),
        name="kproj",
    )(h2, w_kg, b_kg)

    hn_ctx, new_c, new_n, new_m = _mlstm_call(
        z, kt, gates_t, ml_norm, None,
        n_seq=bp, t_len=tp, n_sub=CTX_SEQS, tok_block0=0, dh=dh, unroll=4)
    (hn_lat,) = _mlstm_call(
        z, kt, gates_t, ml_norm, (state_C, state_n, state_m),
        n_seq=bs, t_len=ts, n_sub=1, tok_block0=n_ctx // ts, dh=dh, unroll=4)

    ncb_m = n_ctx // TM_MIX
    tokm = pl.BlockSpec((TM_MIX, dm), lambda i: (i, 0))

    def zcol(j):
        return pl.BlockSpec((TM_MIX, dm), lambda i: (i, j))

    x2 = pl.pallas_call(
        functools.partial(_mixout_kernel, n_ctx_blocks=ncb_m, blocks_per_seq=ts // TM_MIX,
                          ctx_seg=tp),
        grid=(n_tok // TM_MIX,),
        in_specs=[pl.BlockSpec((TM_MIX, dm), lambda i: (jnp.minimum(i, ncb_m - 1), 0)),
                  pl.BlockSpec((TM_MIX, dm), lambda i: (jnp.maximum(i - ncb_m, 0), 0)),
                  zcol(ZO), zcol(ZB), zcol(ZC), zcol(ZX), zcol(ZGM), zcol(ZGS), tokm,
                  mod_spec, _resident((3, dm)), _resident((1, dm)),
                  _resident((dm, dm)), _resident((dm, dm)), _resident((dm, dm))],
        out_specs=tokm,
        out_shape=jax.ShapeDtypeStruct((n_tok, dm), F32),
        compiler_params=_params(1),
        name="mixout",
    )(hn_ctx, hn_lat, z, z, z, z, z, z, x1, mod, conv_w[0],
      conv_b[0][None, :], w_ml_out[0].astype(BF16), w_sc_out[0].astype(BF16), w_o[0].astype(BF16))

    y_p, y_s = pl.pallas_call(
        functools.partial(_ffn2_kernel, chunks=chunks, n_ctx_blocks=ncb,
                          blocks_per_seq=ts // TM_FFN),
        grid=(FFN_WSTEPS + nb,),
        in_specs=[tok_spec, mod_spec, _resident((3, dm)), _resident((1, dm))]
        + ffn_w_specs,
        out_specs=[pl.BlockSpec((TM_FFN, dm), lambda i: (jnp.minimum(tok(i), ncb - 1), 0)),
                   pl.BlockSpec((TM_FFN, dm), lambda i: (jnp.maximum(tok(i) - ncb, 0), 0))],
        out_shape=[jax.ShapeDtypeStruct((n_ctx, dm), F32), jax.ShapeDtypeStruct((n_lat, dm), F32)],
        scratch_shapes=ffn_w_scratch,
        compiler_params=_params(1),
        name="ffn2",
    )(x2, mod, norm_g[0], final_norm[None, :], ffn2_w1[0], ffn2_w2[0])

    return (y_p.reshape(bp, tp, dm), y_s.reshape(bs, ts, dm), new_c,
            new_n, new_m[..., 0, 0].reshape(bp, 1, N_DIR, NH))
```

```python
import functools

import jax
import jax.numpy as jnp
from jax import lax
from jax.experimental import pallas as pl
from jax.experimental.pallas import tpu as pltpu

F32 = jnp.float32
BF16 = jnp.bfloat16

NH = 4
N_DIR = 2
N_MOD = 9
GRID_W = 64
EPS = 1e-6
LANES = 128
SUBLANES = 8
MXU_TILE = 256
VMEM_LIMIT = 56 * 1024 * 1024

TM_FFN = 512
FFN_SUB = 256
FFN_WSTEPS = 11
TM_MIX = 512
MIX_SUB = 256
TM_IN = 2048
TN_IN = 1024
TM_K = 1024
CHUNK = 256
CTX_SEQS = 8
ZQ, ZV, ZO, ZB, ZC, ZX, ZGM, ZGS = range(8)
SIG_GROUPS = (ZO, ZGM, ZGS)


def _params(n_axes):
    return pltpu.CompilerParams(dimension_semantics=("arbitrary",) * n_axes,
                                vmem_limit_bytes=VMEM_LIMIT)


def _resident(shape):
    zeros = (0,) * len(shape)
    return pl.BlockSpec(shape, lambda *_: zeros, pipeline_mode=pl.Buffered(1))


def _rms(x, g):
    return x * lax.rsqrt(jnp.mean(x * x, axis=-1, keepdims=True) + EPS) * g


def _bdot(a, b):
    return jnp.dot(a, b, preferred_element_type=F32)


def _mod_rows(mod_ref, t, n_ctx_blocks, blocks_per_seq):
    r = jnp.where(t < n_ctx_blocks, 0, 1 + (t - n_ctx_blocks) // blocks_per_seq)
    return [mod_ref[k, pl.ds(r, 1), :] for k in range(N_MOD)]


def _swiglu_residual(x, mod, g_norm, w1_ref, w2_ref, chunks, row0):
    d_ff = w2_ref.shape[0]
    h = (_rms(x, g_norm) * (1.0 + mod[row0 + 1]) + mod[row0]).astype(BF16)
    acc = None
    for lo, hi in chunks:
        a = _bdot(h, w1_ref[:, lo:hi])
        b = _bdot(h, w1_ref[:, d_ff + lo:d_ff + hi])
        part = _bdot((jax.nn.silu(a) * b).astype(BF16), w2_ref[lo:hi, :])
        acc = part if acc is None else acc + part
    return x + (0.5 * mod[row0 + 2]) * acc


def _row_parts(tm):
    return [pl.ds(r, FFN_SUB) for r in range(0, tm, FFN_SUB)]


def _stage_weights(i, w1f_ref, w2f_ref, w1_sc, w2_sc, also=None):
    c1, c2 = w1f_ref.shape[1], w2f_ref.shape[0]
    for c in range(FFN_WSTEPS):
        @pl.when(i == c)
        def _(c=c):
            w1_sc[:, c * c1:(c + 1) * c1] = w1f_ref[...].astype(BF16)
            w2_sc[c * c2:(c + 1) * c2, :] = w2f_ref[...].astype(BF16)
            if also is not None:
                also(c)


def _ffn1_kernel(xp_ref, xs_ref, condT_ref, wmod_ref, bmod_ref, g_ref, w1f_ref, w2f_ref,
                 x1_ref, h2_ref, mod_ref, w1_sc, w2_sc, *,
                 chunks, n_ctx_blocks, blocks_per_seq, n_cond):
    i = pl.program_id(0)

    def mod_chunk(c):
        if c < N_MOD:
            s = jax.nn.silu(condT_ref[...])
            w = wmod_ref[...]
            rows = [jnp.sum(w * s[:, r:r + 1], axis=0, keepdims=True) for r in range(n_cond)]
            rows.append(jnp.zeros((SUBLANES - n_cond, w.shape[1]), F32))
            mod_ref[c] = jnp.concatenate(rows, axis=0) + bmod_ref[c:c + 1, :]

    _stage_weights(i, w1f_ref, w2f_ref, w1_sc, w2_sc, also=mod_chunk)

    @pl.when(i >= FFN_WSTEPS)
    def _():
        t = i - FFN_WSTEPS
        is_ctx = t < n_ctx_blocks
        mod = _mod_rows(mod_ref, t, n_ctx_blocks, blocks_per_seq)
        for rows in _row_parts(x1_ref.shape[0]):
            x = jnp.where(is_ctx, xp_ref[rows, :], xs_ref[rows, :])
            x1 = _swiglu_residual(x, mod, g_ref[0:1], w1_sc, w2_sc, chunks, 0)
            x1_ref[rows, :] = x1
            h2_ref[rows, :] = (_rms(x1, g_ref[1:2]) * (1.0 + mod[4]) + mod[3]).astype(BF16)


def _ffn2_kernel(x_ref, mod_ref, g_ref, fin_ref, w1f_ref, w2f_ref, yp_ref, ys_ref, w1_sc, w2_sc, *,
                 chunks, n_ctx_blocks, blocks_per_seq):
    i = pl.program_id(0)
    _stage_weights(i, w1f_ref, w2f_ref, w1_sc, w2_sc)

    @pl.when(i >= FFN_WSTEPS)
    def _():
        mod = _mod_rows(mod_ref, i - FFN_WSTEPS, n_ctx_blocks, blocks_per_seq)
        y = jnp.concatenate(
            [_rms(_swiglu_residual(x_ref[rows, :], mod, g_ref[2:3], w1_sc, w2_sc, chunks, 6),
                  fin_ref[...]) for rows in _row_parts(x_ref.shape[0])], axis=0)

        @pl.when(i - FFN_WSTEPS < n_ctx_blocks)
        def _():
            yp_ref[...] = y

        @pl.when(i - FFN_WSTEPS >= n_ctx_blocks)
        def _():
            ys_ref[...] = y


def _ff_chunks(d_ff):
    tile = MXU_TILE
    half = max(tile, (d_ff // 2 + tile - 1) // tile * tile)
    return ((0, half), (half, d_ff)) if half < d_ff else ((0, d_ff),)


def _inproj_kernel(h_ref, wa_ref, wb_ref, b_ref, z_ref, w_sc, *, n_lead, shift):
    j, i = pl.program_id(0), pl.program_id(1)

    @pl.when(jnp.logical_and(i == 0, j < n_lead))
    def _():
        w_sc[...] = wa_ref[...].astype(BF16)

    @pl.when(jnp.logical_and(i == 0, j >= n_lead))
    def _():
        w_sc[...] = jnp.concatenate([wa_ref[shift:, :], wb_ref[:shift, :]], axis=0).astype(BF16)

    def project():
        return lax.dot_general(h_ref[...], w_sc[...], (((1,), (1,)), ((), ())),
                               preferred_element_type=F32) + b_ref[...]

    is_sig = functools.reduce(jnp.logical_or, [j == g for g in SIG_GROUPS])

    @pl.when(is_sig)
    def _():
        z_ref[...] = (0.5 * jnp.tanh(0.5 * project()) + 0.5).astype(BF16)

    @pl.when(jnp.logical_not(is_sig))
    def _():
        z_ref[...] = project().astype(BF16)


def _kproj_kernel(h_ref, wk_ref, wg_ref, b_ref, kt_ref, gt_ref, wt_sc, *, k_scale):
    dm = kt_ref.shape[1]

    @pl.when(pl.program_id(0) == 0)
    def _():
        wt_sc[:dm, :] = (wk_ref[...] * k_scale).astype(BF16)
        wt_sc[dm:, :] = wg_ref[...].astype(BF16)

    full = lax.dot_general(wt_sc[...], h_ref[...], (((1,), (1,)), ((), ())),
                           preferred_element_type=F32) + b_ref[...]
    for c in range(kt_ref.shape[0]):
        kt_ref[c] = full[:dm, c * CHUNK:(c + 1) * CHUNK].astype(BF16)
    gt_ref[...] = full[dm:, :]


def _lane_scan(x, op, fill, reverse):
    n = x.shape[1]
    lane = lax.broadcasted_iota(jnp.int32, x.shape, 1)
    sh = 1
    while sh < n:
        if reverse:
            nb = jnp.where(lane < n - sh, pltpu.roll(x, n - sh, axis=1), fill)
        else:
            nb = jnp.where(lane >= sh, pltpu.roll(x, sh, axis=1), fill)
        x = op(x, nb)
        sh *= 2
    return x


def _mlstm_kernel(*refs, nc, n_sub, has_init, write_state, unroll):
    it = iter(refs)
    q_ref, kt_ref, v_ref, gr_ref, mln_ref = (next(it) for _ in range(5))
    if has_init:
        c0_ref, n0_ref, m0_ref = next(it), next(it), next(it)
    hn_ref = next(it)
    if write_state:
        cout_ref, nout_ref, mout_ref = next(it), next(it), next(it)
    c_sc, u_sc, wi_sc, cl_sc, wk_sc, vr_sc, dc_sc, mi_sc, mf_sc = (next(it) for _ in range(9))
    if nc > 1:
        h_sc = next(it)

    def head_norm(h):
        mean_sq = _bdot((h * h).astype(BF16), jnp.full((h.shape[1],) * 2, 1.0 / h.shape[1], BF16))
        return (h * lax.rsqrt(mean_sq + EPS) * mln_ref[...]).astype(BF16)

    L = CHUNK
    dh = q_ref.shape[1]
    rows = u_sc.shape[1]
    blk, h_idx = pl.program_id(0), pl.program_id(1)
    row = lax.broadcasted_iota(jnp.int32, (L, L), 0)
    col = lax.broadcasted_iota(jnp.int32, (L, L), 1)
    visible = (col <= row, col >= row)
    no_inbound = (not has_init) and nc == 1
    ones_ext = jnp.ones((L, LANES), BF16)

    def gate_rows(gate, d, hh):
        g_row = (gate * N_DIR + d) * NH + hh
        tokens = gr_ref[g_row:g_row + 1, :]
        return jnp.concatenate([tokens[:, r * L:(r + 1) * L] for r in range(rows)], axis=0)

    def gate_quantities(d, hh):
        rev = d == 1
        k = d * NH + hh
        ig = gate_rows(0, d, hh)
        lf = jax.nn.log_sigmoid(gate_rows(1, d, hh))
        b = _lane_scan(lf, jnp.add, 0.0, rev)
        tot = jnp.broadcast_to(b[:, 0:1] if rev else b[:, L - 1:L], (rows, L))
        vrow = ig - b
        cm = _lane_scan(vrow, jnp.maximum, -jnp.inf, rev)
        g = tot - b + ig
        gmax = jnp.broadcast_to(jnp.max(g, axis=1, keepdims=True), (rows, L))
        mi_sc[k] = jnp.zeros((rows, L), F32)
        mf_sc[k] = jnp.zeros(mf_sc.shape[1:], F32)
        for jj in range(n_sub):
            if has_init:
                m = jnp.full((1, L), m0_ref[((blk * n_sub + jj) * N_DIR + d) * NH + hh], F32)
            else:
                m = jnp.zeros((1, L), F32)
            for c in (range(nc - 1, -1, -1) if rev else range(nc)):
                r = jj * nc + c
                mi_sc[k, r:r + 1, :] = m
                m = jnp.maximum(tot[r:r + 1] + m, gmax[r:r + 1])
            mf_sc[k, jj:jj + 1, :] = m
        m_in = mi_sc[k]
        m_out = jnp.maximum(tot + m_in, gmax)
        mm = jnp.maximum(m_in, cm)
        u_sc[k] = -mm
        wi_sc[k] = jnp.exp(m_in - mm)
        cl_sc[k] = jnp.exp(-(b + mm))
        wk_sc[k] = jnp.exp(g - m_out)
        vr_sc[k] = vrow
        dc_sc[k] = jnp.exp(tot + m_in - m_out)

    @pl.when(h_idx == 0)
    def _():
        for d in range(N_DIR):
            for hh in range(NH):
                gate_quantities(d, hh)

    def chain(d, r, r0, s_raw):
        one = pl.ds(r, 1)
        k = d * NH + h_idx
        qc = q_ref[pl.ds(r0, L), :]
        ktc = kt_ref[r]
        v_ext = jnp.concatenate([v_ref[pl.ds(r0, L), :], ones_ext], axis=1)
        if s_raw is None:
            s_raw = _bdot(qc, ktc)
        def per_query(row_ref, width):
            col = jnp.broadcast_to(row_ref[k, one, :], (LANES, L)).T
            return jnp.concatenate([col] * (width // LANES), axis=1)

        s = s_raw * jnp.exp(jnp.where(visible[d], per_query(u_sc, L) + vr_sc[k, one, :], -jnp.inf))
        acc = _bdot(s.astype(BF16), v_ext)
        if not no_inbound:
            acc = acc + per_query(wi_sc, dh + LANES) * _bdot(qc, c_sc[d].astype(BF16))
        rden = 1.0 / jnp.maximum(jnp.abs(acc[:, dh:]), per_query(cl_sc, LANES))
        h = acc[:, :dh] * jnp.concatenate([rden] * (dh // LANES), axis=1)
        kw = (ktc.astype(F32) * wk_sc[k, one, :]).astype(BF16)
        kv = _bdot(kw, v_ext)
        if no_inbound:
            c_sc[d] = kv
        else:
            c_sc[d] = dc_sc[k, one, :][:, 0:1] * c_sc[d] + kv
        return h

    def do_seq(j):
        base = j * (nc * L)
        for d in range(N_DIR):
            if has_init:
                n_rep = jnp.broadcast_to(n0_ref[j, 0, d, pl.ds(h_idx, 1), :], (LANES, dh)).T
                c_sc[d] = jnp.concatenate([c0_ref[j, 0, d, 0], n_rep], axis=1)
            elif not no_inbound:
                c_sc[d] = jnp.zeros(c_sc.shape[1:], F32)

        if nc == 1:
            r0 = pl.multiple_of(base, L)
            s_raw = _bdot(q_ref[pl.ds(r0, L), :], kt_ref[j])
            hn_ref[pl.ds(r0, L), :] = head_norm(chain(0, j, r0, s_raw) + chain(1, j, r0, s_raw))
        else:
            def step(i, first_visit):
                for d in range(N_DIR):
                    c = i if d == 0 else nc - 1 - i
                    r0 = pl.multiple_of(base + c * L, L)
                    h = chain(d, j * nc + c, r0, None)
                    if first_visit:
                        h_sc[pl.ds(r0, L), :] = h
                    else:
                        hn_ref[pl.ds(r0, L), :] = head_norm(h_sc[pl.ds(r0, L), :] + h)

            lax.fori_loop(0, nc // 2, lambda i, carry: (step(i, True), carry)[1], 0,
                          unroll=unroll)
            lax.fori_loop(nc // 2, nc, lambda i, carry: (step(i, False), carry)[1], 0,
                          unroll=unroll)

        if write_state:
            for d in range(N_DIR):
                cout_ref[j, 0, d, 0] = c_sc[d, :, :dh]
                nout_ref[j, 0, d, pl.ds(h_idx, 1), :] = c_sc[d, :, dh:].T[0:1, :]
                mout_ref[j, d, 0] = mf_sc[d * NH + h_idx, pl.ds(j, 1), :][:, :LANES]

    if n_sub == 1:
        do_seq(0)
    else:
        lax.fori_loop(0, n_sub, lambda j, carry: (do_seq(j), carry)[1], 0, unroll=unroll)


def _mlstm_call(z, kt, gates_t, ml_norm, state, *, n_seq, t_len, n_sub, tok_block0, dh, unroll):
    nc = t_len // CHUNK
    rows = n_sub * nc
    assert rows % SUBLANES == 0 and (nc == 1 or nc % 2 == 0)
    has_init = state is not None
    write_state = not has_init
    kern = functools.partial(_mlstm_kernel, nc=nc, n_sub=n_sub, has_init=has_init,
                             write_state=write_state, unroll=unroll)
    tb = n_sub * t_len

    def z_spec(group):
        return pl.BlockSpec((tb, dh), lambda b, h: (tok_block0 + b, group * NH + h))

    in_specs = [z_spec(ZQ),
                pl.BlockSpec((rows, dh, CHUNK), lambda b, h: (tok_block0 + b, h, 0)),
                z_spec(ZV),
                pl.BlockSpec((gates_t.shape[0], tb), lambda b, h: (0, tok_block0 + b)),
                pl.BlockSpec((1, dh), lambda b, h: (0, h))]
    args = [z, kt, z, gates_t, ml_norm]
    if has_init:
        c0, n0, m0 = state
        in_specs += [pl.BlockSpec((n_sub, 1, N_DIR, 1, dh, dh), lambda b, h: (b, 0, 0, h, 0, 0)),
                     pl.BlockSpec((n_sub, 1, N_DIR, NH, dh), lambda b, h: (b, 0, 0, 0, 0)),
                     pl.BlockSpec(memory_space=pltpu.SMEM)]
        args += [c0, n0, m0.reshape(-1)]
    out_specs = [pl.BlockSpec((tb, dh), lambda b, h: (b, h))]
    out_shape = [jax.ShapeDtypeStruct((n_seq * t_len, NH * dh), BF16)]
    if write_state:
        out_specs += [pl.BlockSpec((n_sub, 1, N_DIR, 1, dh, dh), lambda b, h: (b, 0, 0, h, 0, 0)),
                      pl.BlockSpec((n_sub, 1, N_DIR, NH, dh), lambda b, h: (b, 0, 0, 0, 0)),
                      pl.BlockSpec((n_sub, N_DIR, 1, 1, LANES), lambda b, h: (b, 0, h, 0, 0))]
        out_shape += [jax.ShapeDtypeStruct((n_seq, 1, N_DIR, NH, dh, dh), F32),
                      jax.ShapeDtypeStruct((n_seq, 1, N_DIR, NH, dh), F32),
                      jax.ShapeDtypeStruct((n_seq, N_DIR, NH, 1, LANES), F32)]
    per_row = pltpu.VMEM((N_DIR * NH, rows, CHUNK), F32)
    return pl.pallas_call(
        kern,
        grid=(n_seq // n_sub, NH),
        in_specs=in_specs,
        out_specs=out_specs,
        out_shape=out_shape,
        scratch_shapes=[pltpu.VMEM((N_DIR, dh, dh + LANES), F32)] + [per_row] * 7
        + [pltpu.VMEM((N_DIR * NH, max(SUBLANES, n_sub), CHUNK), F32)]
        + ([pltpu.VMEM((tb, dh), F32)] if nc > 1 else []),
        compiler_params=_params(2),
        name="mlstm_lat" if has_init else "mlstm_ctx",
    )(*args)


def _mixout_kernel(hnc_ref, hnl_ref, zo_ref, zb_ref, zc_ref, zx_ref, zgm_ref, zgs_ref, x1_ref,
                   mod_ref, cw_ref, cb_ref, wml32_ref, wsc32_ref, wo32_ref, x2_ref,
                   wml_ref, wsc_ref, wo_ref, *, n_ctx_blocks, blocks_per_seq, ctx_seg):
    @pl.when(pl.program_id(0) == 0)
    def _():
        wml_ref[...] = wml32_ref[...].astype(BF16)
        wsc_ref[...] = wsc32_ref[...].astype(BF16)
        wo_ref[...] = wo32_ref[...].astype(BF16)

    is_ctx = pl.program_id(0) < n_ctx_blocks
    gate = _mod_rows(mod_ref, pl.program_id(0), n_ctx_blocks, blocks_per_seq)[5]
    sub = MIX_SUB
    t = lax.broadcasted_iota(jnp.int32, (sub, 1), 0)
    pos = jnp.where(is_ctx, t % ctx_seg, t % GRID_W)
    has_prev = pos != 0
    has_next = pos != jnp.where(is_ctx, ctx_seg - 1, GRID_W - 1)
    for r in range(0, x1_ref.shape[0], sub):
        rows = pl.ds(r, sub)
        hn = jnp.where(is_ctx, hnc_ref[rows, :], hnl_ref[rows, :])
        ml = _bdot(zo_ref[rows, :] * hn, wml_ref[...])

        u = zc_ref[rows, :].astype(F32) * zx_ref[rows, :].astype(F32)
        u_prev = jnp.where(has_prev, pltpu.roll(u, 1, axis=0), 0.0)
        u_next = jnp.where(has_next, pltpu.roll(u, sub - 1, axis=0), 0.0)
        uc = u_prev * cw_ref[0:1] + u * cw_ref[1:2] + u_next * cw_ref[2:3] + cb_ref[...]
        sc = _bdot(zb_ref[rows, :] * uc.astype(BF16), wsc_ref[...])

        y = zgm_ref[rows, :].astype(F32) * ml + zgs_ref[rows, :].astype(F32) * sc
        x2_ref[rows, :] = x1_ref[rows, :] + gate * _bdot(y.astype(BF16), wo_ref[...])


def kernel(x_prompt, x_sample, state_C, state_n, state_m, c, c_ctx, w_mod, b_mod, norm_g,
           ffn1_w1, ffn1_w2, w_in, b_in, conv_w, conv_b, ml_norm, w_ml_out, w_sc_out, w_o,
           ffn2_w1, ffn2_w2, final_norm):
    assert w_mod.shape[0] == 1, "single trunk layer only"
    bp, tp, dm = x_prompt.shape
    bs, ts, _ = x_sample.shape
    n_ctx, n_lat = bp * tp, bs * ts
    n_tok = n_ctx + n_lat
    d_ff = ffn1_w2.shape[1]
    ml_dim = w_ml_out.shape[1]
    sc_dim = w_sc_out.shape[1]
    dh = ml_dim // NH
    n_gate = 2 * N_DIR * NH
    assert ml_dim == dm and sc_dim == dm and dh % MXU_TILE == 0 and dm == TN_IN
    assert w_in.shape[2] == 9 * dm + n_gate and n_gate < LANES
    assert n_ctx % TM_FFN == 0 and ts % TM_FFN == 0 and n_tok % TM_IN == 0
    assert n_ctx % TM_MIX == 0 and ts % TM_MIX == 0 and MIX_SUB % tp == 0 and MIX_SUB % GRID_W == 0
    assert TM_K % CHUNK == 0 and n_tok % TM_K == 0
    assert tp == CHUNK and ts % (2 * CHUNK) == 0 and bp % CTX_SEQS == 0 and n_ctx % ts == 0
    n_cond = 1 + bs

    cond = jnp.concatenate([c_ctx[None, :], c], axis=0)
    condT = jnp.pad(cond.T, ((0, 0), (0, SUBLANES - n_cond)))
    mod_spec = _resident((N_MOD, SUBLANES, dm))

    assert w_mod.shape[2] == N_MOD * dm and N_MOD <= FFN_WSTEPS
    w_in0, b_in0 = w_in[0], b_in.reshape(-1)
    gate_lo = 4 * ml_dim
    ncb = n_ctx // TM_FFN
    nb = n_tok // TM_FFN
    chunks = _ff_chunks(d_ff)
    wc1, wc2 = 2 * d_ff // FFN_WSTEPS, d_ff // FFN_WSTEPS
    assert wc1 * FFN_WSTEPS == 2 * d_ff and wc1 % LANES == 0
    assert wc2 * FFN_WSTEPS == d_ff and wc2 % (2 * SUBLANES) == 0

    def tok(i):
        return jnp.maximum(i - FFN_WSTEPS, 0)

    def wstep(i):
        return jnp.minimum(i, FFN_WSTEPS - 1)

    ffn_w_specs = [pl.BlockSpec((dm, wc1), lambda i: (0, wstep(i))),
                   pl.BlockSpec((wc2, dm), lambda i: (wstep(i), 0))]
    ffn_w_scratch = [pltpu.VMEM((dm, 2 * d_ff), BF16), pltpu.VMEM((d_ff, dm), BF16)]
    tok_spec = pl.BlockSpec((TM_FFN, dm), lambda i: (tok(i), 0))
    x1, h2, mod = pl.pallas_call(
        functools.partial(_ffn1_kernel, chunks=chunks, n_ctx_blocks=ncb,
                          blocks_per_seq=ts // TM_FFN, n_cond=n_cond),
        grid=(FFN_WSTEPS + nb,),
        in_specs=[pl.BlockSpec((TM_FFN, dm), lambda i: (jnp.minimum(tok(i), ncb - 1), 0)),
                  pl.BlockSpec((TM_FFN, dm), lambda i: (jnp.maximum(tok(i) - ncb, 0), 0)),
                  _resident((dm, SUBLANES)),
                  pl.BlockSpec((dm, dm), lambda i: (0, jnp.minimum(i, N_MOD - 1))),
                  _resident((N_MOD, dm)), _resident((3, dm))] + ffn_w_specs,
        out_specs=[tok_spec, tok_spec,
                   pl.BlockSpec((N_MOD, SUBLANES, dm), lambda i: (0, 0, 0))],
        out_shape=[jax.ShapeDtypeStruct((n_tok, dm), F32), jax.ShapeDtypeStruct((n_tok, dm), BF16),
                   jax.ShapeDtypeStruct((N_MOD, SUBLANES, dm), F32)],
        scratch_shapes=ffn_w_scratch,
        compiler_params=_params(1),
        name="ffn1",
    )(x_prompt.reshape(n_ctx, dm), x_sample.reshape(n_lat, dm), condT, w_mod[0],
      b_mod.reshape(N_MOD, dm), norm_g[0], ffn1_w1[0], ffn1_w2[0])

    w_in_t = w_in0.T
    n_lead = gate_lo // TN_IN - 1
    b_main = jnp.concatenate([b_in0[:ml_dim], b_in0[2 * ml_dim:gate_lo],
                              b_in0[gate_lo + n_gate:]])[None, :]
    n_main = b_main.shape[1]

    def w_block(j):
        return jnp.where(j == 0, 0, j + 1)

    z = pl.pallas_call(
        functools.partial(_inproj_kernel, n_lead=n_lead, shift=n_gate),
        grid=(n_main // TN_IN, n_tok // TM_IN),
        in_specs=[pl.BlockSpec((TM_IN, dm), lambda j, i: (i, 0)),
                  pl.BlockSpec((TN_IN, dm), lambda j, i: (w_block(j), 0)),
                  pl.BlockSpec((TN_IN, dm), lambda j, i: (w_block(j) + 1, 0)),
                  pl.BlockSpec((1, TN_IN), lambda j, i: (0, j))],
        out_specs=pl.BlockSpec((TM_IN, TN_IN), lambda j, i: (i, j)),
        out_shape=jax.ShapeDtypeStruct((n_tok, n_main), BF16),
        scratch_shapes=[pltpu.VMEM((TN_IN, dm), BF16)],
        compiler_params=_params(2),
        name="inproj",
    )(h2, w_in_t, w_in_t, b_main)

    kscale = dh ** -0.5
    assert ml_dim % dm == 0 and gate_lo % n_gate == 0
    b_kg = jnp.concatenate([b_in0[ml_dim:2 * ml_dim] * kscale,
                            b_in0[gate_lo:gate_lo + n_gate]])[:, None]
    kt, gates_t = pl.pallas_call(
        functools.partial(_kproj_kernel, k_scale=kscale),
        grid=(n_tok // TM_K,),
        in_specs=[pl.BlockSpec((TM_K, dm), lambda i: (i, 0)),
                  pl.BlockSpec((dm, dm), lambda i: (ml_dim // dm, 0),
                               pipeline_mode=pl.Buffered(1)),
                  pl.BlockSpec((n_gate, dm), lambda i: (gate_lo // n_gate, 0),
                               pipeline_mode=pl.Buffered(1)),
                  _resident((dm + n_gate, 1))],
        out_specs=[pl.BlockSpec((TM_K // CHUNK, dm, CHUNK), lambda i: (i, 0, 0)),
                   pl.BlockSpec((n_gate, TM_K), lambda i: (0, i))],
        out_shape=[jax.ShapeDtypeStruct((n_tok // CHUNK, dm, CHUNK), BF16),
                   jax.ShapeDtypeStruct((n_gate, n_tok), F32)],
        scratch_shapes=[pltpu.VMEM((dm + n_gate, dm), BF16)],
        compiler_params=_params(1),
        name="kproj",
    )(h2, w_in_t, w_in_t, b_kg)

    hn_ctx, new_c, new_n, new_m = _mlstm_call(
        z, kt, gates_t, ml_norm, None,
        n_seq=bp, t_len=tp, n_sub=CTX_SEQS, tok_block0=0, dh=dh, unroll=4)
    (hn_lat,) = _mlstm_call(
        z, kt, gates_t, ml_norm, (state_C, state_n, state_m),
        n_seq=bs, t_len=ts, n_sub=1, tok_block0=n_ctx // ts, dh=dh, unroll=4)

    ncb_m = n_ctx // TM_MIX
    tokm = pl.BlockSpec((TM_MIX, dm), lambda i: (i, 0))

    def zcol(j):
        return pl.BlockSpec((TM_MIX, dm), lambda i: (i, j))

    x2 = pl.pallas_call(
        functools.partial(_mixout_kernel, n_ctx_blocks=ncb_m, blocks_per_seq=ts // TM_MIX,
                          ctx_seg=tp),
        grid=(n_tok // TM_MIX,),
        in_specs=[pl.BlockSpec((TM_MIX, dm), lambda i: (jnp.minimum(i, ncb_m - 1), 0)),
                  pl.BlockSpec((TM_MIX, dm), lambda i: (jnp.maximum(i - ncb_m, 0), 0)),
                  zcol(ZO), zcol(ZB), zcol(ZC), zcol(ZX), zcol(ZGM), zcol(ZGS), tokm,
                  mod_spec, _resident((3, dm)), _resident((1, dm)),
                  _resident((dm, dm)), _resident((dm, dm)), _resident((dm, dm))],
        out_specs=tokm,
        out_shape=jax.ShapeDtypeStruct((n_tok, dm), F32),
        scratch_shapes=[pltpu.VMEM((dm, dm), BF16)] * 3,
        compiler_params=_params(1),
        name="mixout",
    )(hn_ctx, hn_lat, z, z, z, z, z, z, x1, mod, conv_w[0],
      conv_b[0][None, :], w_ml_out[0], w_sc_out[0], w_o[0])

    y_p, y_s = pl.pallas_call(
        functools.partial(_ffn2_kernel, chunks=chunks, n_ctx_blocks=ncb,
                          blocks_per_seq=ts // TM_FFN),
        grid=(FFN_WSTEPS + nb,),
        in_specs=[tok_spec, mod_spec, _resident((3, dm)), _resident((1, dm))]
        + ffn_w_specs,
        out_specs=[pl.BlockSpec((TM_FFN, dm), lambda i: (jnp.minimum(tok(i), ncb - 1), 0)),
                   pl.BlockSpec((TM_FFN, dm), lambda i: (jnp.maximum(tok(i) - ncb, 0), 0))],
        out_shape=[jax.ShapeDtypeStruct((n_ctx, dm), F32), jax.ShapeDtypeStruct((n_lat, dm), F32)],
        scratch_shapes=ffn_w_scratch,
        compiler_params=_params(1),
        name="ffn2",
    )(x2, mod, norm_g[0], final_norm[None, :], ffn2_w1[0], ffn2_w2[0])

    return (y_p.reshape(bp, tp, dm), y_s.reshape(bs, ts, dm), new_c,
            new_n, new_m[..., 0, 0].reshape(bp, 1, N_DIR, NH))
```

```python
import functools

import jax
import jax.numpy as jnp
from jax import lax
from jax.experimental import pallas as pl
from jax.experimental.pallas import tpu as pltpu

F32 = jnp.float32
BF16 = jnp.bfloat16

NH = 4
N_DIR = 2
N_MOD = 9
GRID_W = 64
EPS = 1e-6
LANES = 128
SUBLANES = 8
MXU_TILE = 256
VMEM_LIMIT = 56 * 1024 * 1024

TM_FFN = 512
FFN_SUB = 256
FFN_WSTEPS = 11
TM_MIX = 512
MIX_SUB = 256
TM_IN = 2048
TN_IN = 1024
TM_K = 1024
CHUNK = 256
CTX_SEQS = 8
ZQ, ZV, ZO, ZB, ZC, ZX, ZGM, ZGS = range(8)
SIG_GROUPS = (ZO, ZGM, ZGS)


def _params(n_axes):
    return pltpu.CompilerParams(dimension_semantics=("arbitrary",) * n_axes,
                                vmem_limit_bytes=VMEM_LIMIT)


def _resident(shape):
    zeros = (0,) * len(shape)
    return pl.BlockSpec(shape, lambda *_: zeros, pipeline_mode=pl.Buffered(1))


def _rms(x, g):
    return x * lax.rsqrt(jnp.mean(x * x, axis=-1, keepdims=True) + EPS) * g


def _bdot(a, b):
    return jnp.dot(a, b, preferred_element_type=F32)


def _mod_rows(mod_ref, t, n_ctx_blocks, blocks_per_seq):
    r = jnp.where(t < n_ctx_blocks, 0, 1 + (t - n_ctx_blocks) // blocks_per_seq)
    return [mod_ref[k, pl.ds(r, 1), :] for k in range(N_MOD)]


def _swiglu_residual(x, mod, g_norm, w1_ref, w2_ref, chunks, row0):
    d_ff = w2_ref.shape[0]
    h = (_rms(x, g_norm) * (1.0 + mod[row0 + 1]) + mod[row0]).astype(BF16)
    acc = None
    for lo, hi in chunks:
        a = _bdot(h, w1_ref[:, lo:hi])
        b = _bdot(h, w1_ref[:, d_ff + lo:d_ff + hi])
        part = _bdot((jax.nn.silu(a) * b).astype(BF16), w2_ref[lo:hi, :])
        acc = part if acc is None else acc + part
    return x + (0.5 * mod[row0 + 2]) * acc


def _row_parts(tm):
    return [pl.ds(r, FFN_SUB) for r in range(0, tm, FFN_SUB)]


def _stage_weights(i, w1f_ref, w2f_ref, w1_sc, w2_sc, also=None):
    c1, c2 = w1f_ref.shape[1], w2f_ref.shape[0]
    for c in range(FFN_WSTEPS):
        @pl.when(i == c)
        def _(c=c):
            w1_sc[:, c * c1:(c + 1) * c1] = w1f_ref[...].astype(BF16)
            w2_sc[c * c2:(c + 1) * c2, :] = w2f_ref[...].astype(BF16)
            if also is not None:
                also(c)


def _ffn1_kernel(xp_ref, xs_ref, condT_ref, wmod_ref, bmod_ref, g_ref, w1f_ref, w2f_ref,
                 x1_ref, h2_ref, mod_ref, w1_sc, w2_sc, *,
                 chunks, n_ctx_blocks, blocks_per_seq, n_cond):
    i = pl.program_id(0)

    def mod_chunk(c):
        if c < N_MOD:
            s = jax.nn.silu(condT_ref[...])
            w = wmod_ref[...]
            rows = [jnp.sum(w * s[:, r:r + 1], axis=0, keepdims=True) for r in range(n_cond)]
            rows.append(jnp.zeros((SUBLANES - n_cond, w.shape[1]), F32))
            mod_ref[c] = jnp.concatenate(rows, axis=0) + bmod_ref[c:c + 1, :]

    _stage_weights(i, w1f_ref, w2f_ref, w1_sc, w2_sc, also=mod_chunk)

    @pl.when(i >= FFN_WSTEPS)
    def _():
        t = i - FFN_WSTEPS
        is_ctx = t < n_ctx_blocks
        mod = _mod_rows(mod_ref, t, n_ctx_blocks, blocks_per_seq)
        for rows in _row_parts(x1_ref.shape[0]):
            x = jnp.where(is_ctx, xp_ref[rows, :], xs_ref[rows, :])
            x1 = _swiglu_residual(x, mod, g_ref[0:1], w1_sc, w2_sc, chunks, 0)
            x1_ref[rows, :] = x1
            h2_ref[rows, :] = (_rms(x1, g_ref[1:2]) * (1.0 + mod[4]) + mod[3]).astype(BF16)


def _ffn2_kernel(x_ref, mod_ref, g_ref, fin_ref, w1f_ref, w2f_ref, yp_ref, ys_ref, w1_sc, w2_sc, *,
                 chunks, n_ctx_blocks, blocks_per_seq):
    i = pl.program_id(0)
    _stage_weights(i, w1f_ref, w2f_ref, w1_sc, w2_sc)

    @pl.when(i >= FFN_WSTEPS)
    def _():
        mod = _mod_rows(mod_ref, i - FFN_WSTEPS, n_ctx_blocks, blocks_per_seq)
        y = jnp.concatenate(
            [_rms(_swiglu_residual(x_ref[rows, :], mod, g_ref[2:3], w1_sc, w2_sc, chunks, 6),
                  fin_ref[...]) for rows in _row_parts(x_ref.shape[0])], axis=0)

        @pl.when(i - FFN_WSTEPS < n_ctx_blocks)
        def _():
            yp_ref[...] = y

        @pl.when(i - FFN_WSTEPS >= n_ctx_blocks)
        def _():
            ys_ref[...] = y


def _ff_chunks(d_ff):
    tile = MXU_TILE
    half = max(tile, (d_ff // 2 + tile - 1) // tile * tile)
    return ((0, half), (half, d_ff)) if half < d_ff else ((0, d_ff),)


def _inproj_kernel(h_ref, wa_ref, wb_ref, b_ref, z_ref, w_sc, *, n_lead, shift):
    j, i = pl.program_id(0), pl.program_id(1)

    @pl.when(jnp.logical_and(i == 0, j < n_lead))
    def _():
        w_sc[...] = wa_ref[...].astype(BF16)

    @pl.when(jnp.logical_and(i == 0, j >= n_lead))
    def _():
        w_sc[...] = jnp.concatenate([wa_ref[shift:, :], wb_ref[:shift, :]], axis=0).astype(BF16)

    def project():
        return lax.dot_general(h_ref[...], w_sc[...], (((1,), (1,)), ((), ())),
                               preferred_element_type=F32) + b_ref[...]

    is_sig = functools.reduce(jnp.logical_or, [j == g for g in SIG_GROUPS])

    @pl.when(is_sig)
    def _():
        z_ref[...] = (0.5 * jnp.tanh(0.5 * project()) + 0.5).astype(BF16)

    @pl.when(jnp.logical_not(is_sig))
    def _():
        z_ref[...] = project().astype(BF16)


def _kproj_kernel(h_ref, wk_ref, wg_ref, b_ref, kt_ref, gt_ref, wt_sc, *, k_scale):
    dm = kt_ref.shape[1]

    @pl.when(pl.program_id(0) == 0)
    def _():
        wt_sc[:dm, :] = (wk_ref[...] * k_scale).astype(BF16)
        wt_sc[dm:, :] = wg_ref[...].astype(BF16)

    full = lax.dot_general(wt_sc[...], h_ref[...], (((1,), (1,)), ((), ())),
                           preferred_element_type=F32) + b_ref[...]
    for c in range(kt_ref.shape[0]):
        kt_ref[c] = full[:dm, c * CHUNK:(c + 1) * CHUNK].astype(BF16)
    gt_ref[...] = full[dm:, :]


def _lane_scan(x, op, fill, reverse):
    n = x.shape[1]
    lane = lax.broadcasted_iota(jnp.int32, x.shape, 1)
    sh = 1
    while sh < n:
        if reverse:
            nb = jnp.where(lane < n - sh, pltpu.roll(x, n - sh, axis=1), fill)
        else:
            nb = jnp.where(lane >= sh, pltpu.roll(x, sh, axis=1), fill)
        x = op(x, nb)
        sh *= 2
    return x


def _mlstm_kernel(*refs, nc, n_sub, has_init, write_state, unroll):
    it = iter(refs)
    q_ref, kt_ref, v_ref, gr_ref, mln_ref = (next(it) for _ in range(5))
    if has_init:
        c0_ref, n0_ref, m0_ref = next(it), next(it), next(it)
    hn_ref = next(it)
    if write_state:
        cout_ref, nout_ref, mout_ref = next(it), next(it), next(it)
    c_sc, u_sc, wi_sc, cl_sc, wk_sc, vr_sc, dc_sc, mi_sc, mf_sc = (next(it) for _ in range(9))
    if nc > 1:
        h_sc = next(it)

    def head_norm(h):
        mean_sq = _bdot((h * h).astype(BF16), jnp.full((h.shape[1],) * 2, 1.0 / h.shape[1], BF16))
        return (h * lax.rsqrt(mean_sq + EPS) * mln_ref[...]).astype(BF16)

    L = CHUNK
    dh = q_ref.shape[1]
    rows = u_sc.shape[1]
    blk, h_idx = pl.program_id(0), pl.program_id(1)
    row = lax.broadcasted_iota(jnp.int32, (L, L), 0)
    col = lax.broadcasted_iota(jnp.int32, (L, L), 1)
    visible = (col <= row, col >= row)
    no_inbound = (not has_init) and nc == 1
    ones_ext = jnp.ones((L, LANES), BF16)

    def gate_rows(gate, d, hh):
        g_row = (gate * N_DIR + d) * NH + hh
        tokens = gr_ref[g_row:g_row + 1, :]
        return jnp.concatenate([tokens[:, r * L:(r + 1) * L] for r in range(rows)], axis=0)

    def gate_quantities(d, hh):
        rev = d == 1
        k = d * NH + hh
        ig = gate_rows(0, d, hh)
        lf = jax.nn.log_sigmoid(gate_rows(1, d, hh))
        b = _lane_scan(lf, jnp.add, 0.0, rev)
        tot = jnp.broadcast_to(b[:, 0:1] if rev else b[:, L - 1:L], (rows, L))
        vrow = ig - b
        cm = _lane_scan(vrow, jnp.maximum, -jnp.inf, rev)
        g = tot - b + ig
        gmax = jnp.broadcast_to(jnp.max(g, axis=1, keepdims=True), (rows, L))
        mi_sc[k] = jnp.zeros((rows, L), F32)
        mf_sc[k] = jnp.zeros(mf_sc.shape[1:], F32)
        for jj in range(n_sub):
            if has_init:
                m = jnp.full((1, L), m0_ref[((blk * n_sub + jj) * N_DIR + d) * NH + hh], F32)
            else:
                m = jnp.zeros((1, L), F32)
            for c in (range(nc - 1, -1, -1) if rev else range(nc)):
                r = jj * nc + c
                mi_sc[k, r:r + 1, :] = m
                m = jnp.maximum(tot[r:r + 1] + m, gmax[r:r + 1])
            mf_sc[k, jj:jj + 1, :] = m
        m_in = mi_sc[k]
        m_out = jnp.maximum(tot + m_in, gmax)
        mm = jnp.maximum(m_in, cm)
        u_sc[k] = -mm
        wi_sc[k] = jnp.exp(m_in - mm)
        cl_sc[k] = jnp.exp(-(b + mm))
        wk_sc[k] = jnp.exp(g - m_out)
        vr_sc[k] = vrow
        dc_sc[k] = jnp.exp(tot + m_in - m_out)

    @pl.when(h_idx == 0)
    def _():
        for d in range(N_DIR):
            for hh in range(NH):
                gate_quantities(d, hh)

    def chain(d, r, r0, s_raw):
        one = pl.ds(r, 1)
        k = d * NH + h_idx
        qc = q_ref[pl.ds(r0, L), :]
        ktc = kt_ref[r]
        v_ext = jnp.concatenate([v_ref[pl.ds(r0, L), :], ones_ext], axis=1)
        if s_raw is None:
            s_raw = _bdot(qc, ktc)
        def per_query(row_ref, width):
            col = jnp.broadcast_to(row_ref[k, one, :], (LANES, L)).T
            return jnp.concatenate([col] * (width // LANES), axis=1)

        s = s_raw * jnp.exp(jnp.where(visible[d], per_query(u_sc, L) + vr_sc[k, one, :], -jnp.inf))
        num = _bdot(s.astype(BF16), v_ext[:, :dh])
        den = jnp.sum(s[:, :LANES] + s[:, LANES:], axis=1, keepdims=True)
        if not no_inbound:
            inter = per_query(wi_sc, dh + LANES) * _bdot(qc, c_sc[d].astype(BF16))
            num = num + inter[:, :dh]
            den = den + inter[:, dh:]
        rden = 1.0 / jnp.maximum(jnp.abs(den), per_query(cl_sc, LANES))
        h = num * jnp.concatenate([rden] * (dh // LANES), axis=1)
        kw = (ktc.astype(F32) * wk_sc[k, one, :]).astype(BF16)
        kv = _bdot(kw, v_ext)
        if no_inbound:
            c_sc[d] = kv
        else:
            c_sc[d] = dc_sc[k, one, :][:, 0:1] * c_sc[d] + kv
        return h

    def do_seq(j):
        base = j * (nc * L)
        for d in range(N_DIR):
            if has_init:
                n_rep = jnp.broadcast_to(n0_ref[j, 0, d, pl.ds(h_idx, 1), :], (LANES, dh)).T
                c_sc[d] = jnp.concatenate([c0_ref[j, 0, d, 0], n_rep], axis=1)
            elif not no_inbound:
                c_sc[d] = jnp.zeros(c_sc.shape[1:], F32)

        if nc == 1:
            r0 = pl.multiple_of(base, L)
            s_raw = _bdot(q_ref[pl.ds(r0, L), :], kt_ref[j])
            hn_ref[pl.ds(r0, L), :] = head_norm(chain(0, j, r0, s_raw) + chain(1, j, r0, s_raw))
        else:
            def step(i, first_visit):
                for d in range(N_DIR):
                    c = i if d == 0 else nc - 1 - i
                    r0 = pl.multiple_of(base + c * L, L)
                    h = chain(d, j * nc + c, r0, None)
                    if first_visit:
                        h_sc[pl.ds(r0, L), :] = h
                    else:
                        hn_ref[pl.ds(r0, L), :] = head_norm(h_sc[pl.ds(r0, L), :] + h)

            lax.fori_loop(0, nc // 2, lambda i, carry: (step(i, True), carry)[1], 0,
                          unroll=unroll)
            lax.fori_loop(nc // 2, nc, lambda i, carry: (step(i, False), carry)[1], 0,
                          unroll=unroll)

        if write_state:
            for d in range(N_DIR):
                cout_ref[j, 0, d, 0] = c_sc[d, :, :dh]
                nout_ref[j, 0, d, pl.ds(h_idx, 1), :] = c_sc[d, :, dh:].T[0:1, :]
                mout_ref[j, d, 0] = mf_sc[d * NH + h_idx, pl.ds(j, 1), :][:, :LANES]

    if n_sub == 1:
        do_seq(0)
    else:
        lax.fori_loop(0, n_sub, lambda j, carry: (do_seq(j), carry)[1], 0, unroll=unroll)


def _mlstm_call(z, kt, gates_t, ml_norm, state, *, n_seq, t_len, n_sub, tok_block0, dh, unroll):
    nc = t_len // CHUNK
    rows = n_sub * nc
    assert rows % SUBLANES == 0 and (nc == 1 or nc % 2 == 0)
    has_init = state is not None
    write_state = not has_init
    kern = functools.partial(_mlstm_kernel, nc=nc, n_sub=n_sub, has_init=has_init,
                             write_state=write_state, unroll=unroll)
    tb = n_sub * t_len

    def z_spec(group):
        return pl.BlockSpec((tb, dh), lambda b, h: (tok_block0 + b, group * NH + h))

    in_specs = [z_spec(ZQ),
                pl.BlockSpec((rows, dh, CHUNK), lambda b, h: (tok_block0 + b, h, 0)),
                z_spec(ZV),
                pl.BlockSpec((gates_t.shape[0], tb), lambda b, h: (0, tok_block0 + b)),
                pl.BlockSpec((1, dh), lambda b, h: (0, h))]
    args = [z, kt, z, gates_t, ml_norm]
    if has_init:
        c0, n0, m0 = state
        in_specs += [pl.BlockSpec((n_sub, 1, N_DIR, 1, dh, dh), lambda b, h: (b, 0, 0, h, 0, 0)),
                     pl.BlockSpec((n_sub, 1, N_DIR, NH, dh), lambda b, h: (b, 0, 0, 0, 0)),
                     pl.BlockSpec(memory_space=pltpu.SMEM)]
        args += [c0, n0, m0.reshape(-1)]
    out_specs = [pl.BlockSpec((tb, dh), lambda b, h: (b, h))]
    out_shape = [jax.ShapeDtypeStruct((n_seq * t_len, NH * dh), BF16)]
    if write_state:
        out_specs += [pl.BlockSpec((n_sub, 1, N_DIR, 1, dh, dh), lambda b, h: (b, 0, 0, h, 0, 0)),
                      pl.BlockSpec((n_sub, 1, N_DIR, NH, dh), lambda b, h: (b, 0, 0, 0, 0)),
                      pl.BlockSpec((n_sub, N_DIR, 1, 1, LANES), lambda b, h: (b, 0, h, 0, 0))]
        out_shape += [jax.ShapeDtypeStruct((n_seq, 1, N_DIR, NH, dh, dh), F32),
                      jax.ShapeDtypeStruct((n_seq, 1, N_DIR, NH, dh), F32),
                      jax.ShapeDtypeStruct((n_seq, N_DIR, NH, 1, LANES), F32)]
    per_row = pltpu.VMEM((N_DIR * NH, rows, CHUNK), F32)
    return pl.pallas_call(
        kern,
        grid=(n_seq // n_sub, NH),
        in_specs=in_specs,
        out_specs=out_specs,
        out_shape=out_shape,
        scratch_shapes=[pltpu.VMEM((N_DIR, dh, dh + LANES), F32)] + [per_row] * 7
        + [pltpu.VMEM((N_DIR * NH, max(SUBLANES, n_sub), CHUNK), F32)]
        + ([pltpu.VMEM((tb, dh), F32)] if nc > 1 else []),
        compiler_params=_params(2),
        name="mlstm_lat" if has_init else "mlstm_ctx",
    )(*args)


def _mixout_kernel(hnc_ref, hnl_ref, zo_ref, zb_ref, zc_ref, zx_ref, zgm_ref, zgs_ref, x1_ref,
                   mod_ref, cw_ref, cb_ref, wml32_ref, wsc32_ref, wo32_ref, x2_ref,
                   wml_ref, wsc_ref, wo_ref, *, n_ctx_blocks, blocks_per_seq, ctx_seg):
    @pl.when(pl.program_id(0) == 0)
    def _():
        wml_ref[...] = wml32_ref[...].astype(BF16)
        wsc_ref[...] = wsc32_ref[...].astype(BF16)
        wo_ref[...] = wo32_ref[...].astype(BF16)

    is_ctx = pl.program_id(0) < n_ctx_blocks
    gate = _mod_rows(mod_ref, pl.program_id(0), n_ctx_blocks, blocks_per_seq)[5]
    sub = MIX_SUB
    t = lax.broadcasted_iota(jnp.int32, (sub, 1), 0)
    pos = jnp.where(is_ctx, t % ctx_seg, t % GRID_W)
    has_prev = pos != 0
    has_next = pos != jnp.where(is_ctx, ctx_seg - 1, GRID_W - 1)
    for r in range(0, x1_ref.shape[0], sub):
        rows = pl.ds(r, sub)
        hn = jnp.where(is_ctx, hnc_ref[rows, :], hnl_ref[rows, :])
        ml = _bdot(zo_ref[rows, :] * hn, wml_ref[...])

        u = zc_ref[rows, :].astype(F32) * zx_ref[rows, :].astype(F32)
        u_prev = jnp.where(has_prev, pltpu.roll(u, 1, axis=0), 0.0)
        u_next = jnp.where(has_next, pltpu.roll(u, sub - 1, axis=0), 0.0)
        uc = u_prev * cw_ref[0:1] + u * cw_ref[1:2] + u_next * cw_ref[2:3] + cb_ref[...]
        sc = _bdot(zb_ref[rows, :] * uc.astype(BF16), wsc_ref[...])

        y = zgm_ref[rows, :].astype(F32) * ml + zgs_ref[rows, :].astype(F32) * sc
        x2_ref[rows, :] = x1_ref[rows, :] + gate * _bdot(y.astype(BF16), wo_ref[...])


def kernel(x_prompt, x_sample, state_C, state_n, state_m, c, c_ctx, w_mod, b_mod, norm_g,
           ffn1_w1, ffn1_w2, w_in, b_in, conv_w, conv_b, ml_norm, w_ml_out, w_sc_out, w_o,
           ffn2_w1, ffn2_w2, final_norm):
    assert w_mod.shape[0] == 1, "single trunk layer only"
    bp, tp, dm = x_prompt.shape
    bs, ts, _ = x_sample.shape
    n_ctx, n_lat = bp * tp, bs * ts
    n_tok = n_ctx + n_lat
    d_ff = ffn1_w2.shape[1]
    ml_dim = w_ml_out.shape[1]
    sc_dim = w_sc_out.shape[1]
    dh = ml_dim // NH
    n_gate = 2 * N_DIR * NH
    assert ml_dim == dm and sc_dim == dm and dh % MXU_TILE == 0 and dm == TN_IN
    assert w_in.shape[2] == 9 * dm + n_gate and n_gate < LANES
    assert n_ctx % TM_FFN == 0 and ts % TM_FFN == 0 and n_tok % TM_IN == 0
    assert n_ctx % TM_MIX == 0 and ts % TM_MIX == 0 and MIX_SUB % tp == 0 and MIX_SUB % GRID_W == 0
    assert TM_K % CHUNK == 0 and n_tok % TM_K == 0
    assert tp == CHUNK and ts % (2 * CHUNK) == 0 and bp % CTX_SEQS == 0 and n_ctx % ts == 0
    n_cond = 1 + bs

    cond = jnp.concatenate([c_ctx[None, :], c], axis=0)
    condT = jnp.pad(cond.T, ((0, 0), (0, SUBLANES - n_cond)))
    mod_spec = _resident((N_MOD, SUBLANES, dm))

    assert w_mod.shape[2] == N_MOD * dm and N_MOD <= FFN_WSTEPS
    w_in0, b_in0 = w_in[0], b_in.reshape(-1)
    gate_lo = 4 * ml_dim
    ncb = n_ctx // TM_FFN
    nb = n_tok // TM_FFN
    chunks = _ff_chunks(d_ff)
    wc1, wc2 = 2 * d_ff // FFN_WSTEPS, d_ff // FFN_WSTEPS
    assert wc1 * FFN_WSTEPS == 2 * d_ff and wc1 % LANES == 0
    assert wc2 * FFN_WSTEPS == d_ff and wc2 % (2 * SUBLANES) == 0

    def tok(i):
        return jnp.maximum(i - FFN_WSTEPS, 0)

    def wstep(i):
        return jnp.minimum(i, FFN_WSTEPS - 1)

    ffn_w_specs = [pl.BlockSpec((dm, wc1), lambda i: (0, wstep(i))),
                   pl.BlockSpec((wc2, dm), lambda i: (wstep(i), 0))]
    ffn_w_scratch = [pltpu.VMEM((dm, 2 * d_ff), BF16), pltpu.VMEM((d_ff, dm), BF16)]
    tok_spec = pl.BlockSpec((TM_FFN, dm), lambda i: (tok(i), 0))
    x1, h2, mod = pl.pallas_call(
        functools.partial(_ffn1_kernel, chunks=chunks, n_ctx_blocks=ncb,
                          blocks_per_seq=ts // TM_FFN, n_cond=n_cond),
        grid=(FFN_WSTEPS + nb,),
        in_specs=[pl.BlockSpec((TM_FFN, dm), lambda i: (jnp.minimum(tok(i), ncb - 1), 0)),
                  pl.BlockSpec((TM_FFN, dm), lambda i: (jnp.maximum(tok(i) - ncb, 0), 0)),
                  _resident((dm, SUBLANES)),
                  pl.BlockSpec((dm, dm), lambda i: (0, jnp.minimum(i, N_MOD - 1))),
                  _resident((N_MOD, dm)), _resident((3, dm))] + ffn_w_specs,
        out_specs=[tok_spec, tok_spec,
                   pl.BlockSpec((N_MOD, SUBLANES, dm), lambda i: (0, 0, 0))],
        out_shape=[jax.ShapeDtypeStruct((n_tok, dm), F32), jax.ShapeDtypeStruct((n_tok, dm), BF16),
                   jax.ShapeDtypeStruct((N_MOD, SUBLANES, dm), F32)],
        scratch_shapes=ffn_w_scratch,
        compiler_params=_params(1),
        name="ffn1",
    )(x_prompt.reshape(n_ctx, dm), x_sample.reshape(n_lat, dm), condT, w_mod[0],
      b_mod.reshape(N_MOD, dm), norm_g[0], ffn1_w1[0], ffn1_w2[0])

    w_in_t = w_in0.T
    n_lead = gate_lo // TN_IN - 1
    b_main = jnp.concatenate([b_in0[:ml_dim], b_in0[2 * ml_dim:gate_lo],
                              b_in0[gate_lo + n_gate:]])[None, :]
    n_main = b_main.shape[1]

    def w_block(j):
        return jnp.where(j == 0, 0, j + 1)

    z = pl.pallas_call(
        functools.partial(_inproj_kernel, n_lead=n_lead, shift=n_gate),
        grid=(n_main // TN_IN, n_tok // TM_IN),
        in_specs=[pl.BlockSpec((TM_IN, dm), lambda j, i: (i, 0)),
                  pl.BlockSpec((TN_IN, dm), lambda j, i: (w_block(j), 0)),
                  pl.BlockSpec((TN_IN, dm), lambda j, i: (w_block(j) + 1, 0)),
                  pl.BlockSpec((1, TN_IN), lambda j, i: (0, j))],
        out_specs=pl.BlockSpec((TM_IN, TN_IN), lambda j, i: (i, j)),
        out_shape=jax.ShapeDtypeStruct((n_tok, n_main), BF16),
        scratch_shapes=[pltpu.VMEM((TN_IN, dm), BF16)],
        compiler_params=_params(2),
        name="inproj",
    )(h2, w_in_t, w_in_t, b_main)

    kscale = dh ** -0.5
    assert ml_dim % dm == 0 and gate_lo % n_gate == 0
    b_kg = jnp.concatenate([b_in0[ml_dim:2 * ml_dim] * kscale,
                            b_in0[gate_lo:gate_lo + n_gate]])[:, None]
    kt, gates_t = pl.pallas_call(
        functools.partial(_kproj_kernel, k_scale=kscale),
        grid=(n_tok // TM_K,),
        in_specs=[pl.BlockSpec((TM_K, dm), lambda i: (i, 0)),
                  pl.BlockSpec((dm, dm), lambda i: (ml_dim // dm, 0),
                               pipeline_mode=pl.Buffered(1)),
                  pl.BlockSpec((n_gate, dm), lambda i: (gate_lo // n_gate, 0),
                               pipeline_mode=pl.Buffered(1)),
                  _resident((dm + n_gate, 1))],
        out_specs=[pl.BlockSpec((TM_K // CHUNK, dm, CHUNK), lambda i: (i, 0, 0)),
                   pl.BlockSpec((n_gate, TM_K), lambda i: (0, i))],
        out_shape=[jax.ShapeDtypeStruct((n_tok // CHUNK, dm, CHUNK), BF16),
                   jax.ShapeDtypeStruct((n_gate, n_tok), F32)],
        scratch_shapes=[pltpu.VMEM((dm + n_gate, dm), BF16)],
        compiler_params=_params(1),
        name="kproj",
    )(h2, w_in_t, w_in_t, b_kg)

    hn_ctx, new_c, new_n, new_m = _mlstm_call(
        z, kt, gates_t, ml_norm, None,
        n_seq=bp, t_len=tp, n_sub=CTX_SEQS, tok_block0=0, dh=dh, unroll=4)
    (hn_lat,) = _mlstm_call(
        z, kt, gates_t, ml_norm, (state_C, state_n, state_m),
        n_seq=bs, t_len=ts, n_sub=1, tok_block0=n_ctx // ts, dh=dh, unroll=4)

    ncb_m = n_ctx // TM_MIX
    tokm = pl.BlockSpec((TM_MIX, dm), lambda i: (i, 0))

    def zcol(j):
        return pl.BlockSpec((TM_MIX, dm), lambda i: (i, j))

    x2 = pl.pallas_call(
        functools.partial(_mixout_kernel, n_ctx_blocks=ncb_m, blocks_per_seq=ts // TM_MIX,
                          ctx_seg=tp),
        grid=(n_tok // TM_MIX,),
        in_specs=[pl.BlockSpec((TM_MIX, dm), lambda i: (jnp.minimum(i, ncb_m - 1), 0)),
                  pl.BlockSpec((TM_MIX, dm), lambda i: (jnp.maximum(i - ncb_m, 0), 0)),
                  zcol(ZO), zcol(ZB), zcol(ZC), zcol(ZX), zcol(ZGM), zcol(ZGS), tokm,
                  mod_spec, _resident((3, dm)), _resident((1, dm)),
                  _resident((dm, dm)), _resident((dm, dm)), _resident((dm, dm))],
        out_specs=tokm,
        out_shape=jax.ShapeDtypeStruct((n_tok, dm), F32),
        scratch_shapes=[pltpu.VMEM((dm, dm), BF16)] * 3,
        compiler_params=_params(1),
        name="mixout",
    )(hn_ctx, hn_lat, z, z, z, z, z, z, x1, mod, conv_w[0],
      conv_b[0][None, :], w_ml_out[0], w_sc_out[0], w_o[0])

    y_p, y_s = pl.pallas_call(
        functools.partial(_ffn2_kernel, chunks=chunks, n_ctx_blocks=ncb,
                          blocks_per_seq=ts // TM_FFN),
        grid=(FFN_WSTEPS + nb,),
        in_specs=[tok_spec, mod_spec, _resident((3, dm)), _resident((1, dm))]
        + ffn_w_specs,
        out_specs=[pl.BlockSpec((TM_FFN, dm), lambda i: (jnp.minimum(tok(i), ncb - 1), 0)),
                   pl.BlockSpec((TM_FFN, dm), lambda i: (jnp.maximum(tok(i) - ncb, 0), 0))],
        out_shape=[jax.ShapeDtypeStruct((n_ctx, dm), F32), jax.ShapeDtypeStruct((n_lat, dm), F32)],
        scratch_shapes=ffn_w_scratch,
        compiler_params=_params(1),
        name="ffn2",
    )(x2, mod, norm_g[0], final_norm[None, :], ffn2_w1[0], ffn2_w2[0])

    return (y_p.reshape(bp, tp, dm), y_s.reshape(bs, ts, dm), new_c,
            new_n, new_m[..., 0, 0].reshape(bp, 1, N_DIR, NH))
```

```python
import functools

import jax
import jax.numpy as jnp
from jax import lax
from jax.experimental import pallas as pl
from jax.experimental.pallas import tpu as pltpu

F32 = jnp.float32
BF16 = jnp.bfloat16

NH = 4
N_DIR = 2
N_MOD = 9
GRID_W = 64
EPS = 1e-6
LANES = 128
SUBLANES = 8
MXU_TILE = 256
VMEM_LIMIT = 56 * 1024 * 1024

TM_FFN = 512
FFN_SUB = 256
FFN_WSTEPS = 11
TM_MIX = 512
MIX_SUB = 256
TM_IN = 2048
TN_IN = 1024
TM_K = 1024
CHUNK = 256
CTX_SEQS = 8
ZQ, ZV, ZO, ZB, ZC, ZX, ZGM, ZGS = range(8)
SIG_GROUPS = (ZO, ZGM, ZGS)


def _params(n_axes):
    return pltpu.CompilerParams(dimension_semantics=("arbitrary",) * n_axes,
                                vmem_limit_bytes=VMEM_LIMIT)


def _resident(shape):
    zeros = (0,) * len(shape)
    return pl.BlockSpec(shape, lambda *_: zeros, pipeline_mode=pl.Buffered(1))


def _rms(x, g):
    return x * lax.rsqrt(jnp.mean(x * x, axis=-1, keepdims=True) + EPS) * g


def _bdot(a, b):
    return jnp.dot(a, b, preferred_element_type=F32)


def _mod_rows(mod_ref, t, n_ctx_blocks, blocks_per_seq):
    r = jnp.where(t < n_ctx_blocks, 0, 1 + (t - n_ctx_blocks) // blocks_per_seq)
    return [mod_ref[k, pl.ds(r, 1), :] for k in range(N_MOD)]


def _swiglu_residual(x, mod, g_norm, w1_ref, w2_ref, chunks, row0):
    d_ff = w2_ref.shape[0]
    h = (_rms(x, g_norm) * (1.0 + mod[row0 + 1]) + mod[row0]).astype(BF16)
    acc = None
    for lo, hi in chunks:
        a = _bdot(h, w1_ref[:, lo:hi])
        b = _bdot(h, w1_ref[:, d_ff + lo:d_ff + hi])
        half_a = 0.5 * a
        part = _bdot(((half_a + half_a * jnp.tanh(half_a)) * b).astype(BF16), w2_ref[lo:hi, :])
        acc = part if acc is None else acc + part
    return x + (0.5 * mod[row0 + 2]) * acc


def _row_parts(tm):
    return [pl.ds(r, FFN_SUB) for r in range(0, tm, FFN_SUB)]


def _stage_weights(i, w1f_ref, w2f_ref, w1_sc, w2_sc, also=None):
    c1, c2 = w1f_ref.shape[1], w2f_ref.shape[0]
    for c in range(FFN_WSTEPS):
        @pl.when(i == c)
        def _(c=c):
            w1_sc[:, c * c1:(c + 1) * c1] = w1f_ref[...].astype(BF16)
            w2_sc[c * c2:(c + 1) * c2, :] = w2f_ref[...].astype(BF16)
            if also is not None:
                also(c)


def _ffn1_kernel(xp_ref, xs_ref, condT_ref, wmod_ref, bmod_ref, g_ref, w1f_ref, w2f_ref,
                 x1_ref, h2_ref, mod_ref, w1_sc, w2_sc, *,
                 chunks, n_ctx_blocks, blocks_per_seq, n_cond):
    i = pl.program_id(0)

    def mod_chunk(c):
        if c < N_MOD:
            s = jax.nn.silu(condT_ref[...])
            w = wmod_ref[...]
            rows = [jnp.sum(w * s[:, r:r + 1], axis=0, keepdims=True) for r in range(n_cond)]
            rows.append(jnp.zeros((SUBLANES - n_cond, w.shape[1]), F32))
            mod_ref[c] = jnp.concatenate(rows, axis=0) + bmod_ref[c:c + 1, :]

    _stage_weights(i, w1f_ref, w2f_ref, w1_sc, w2_sc, also=mod_chunk)

    @pl.when(i >= FFN_WSTEPS)
    def _():
        t = i - FFN_WSTEPS
        is_ctx = t < n_ctx_blocks
        mod = _mod_rows(mod_ref, t, n_ctx_blocks, blocks_per_seq)
        for rows in _row_parts(x1_ref.shape[0]):
            x = jnp.where(is_ctx, xp_ref[rows, :], xs_ref[rows, :])
            x1 = _swiglu_residual(x, mod, g_ref[0:1], w1_sc, w2_sc, chunks, 0)
            x1_ref[rows, :] = x1
            h2_ref[rows, :] = (_rms(x1, g_ref[1:2]) * (1.0 + mod[4]) + mod[3]).astype(BF16)


def _ffn2_kernel(x_ref, mod_ref, g_ref, fin_ref, w1f_ref, w2f_ref, yp_ref, ys_ref, w1_sc, w2_sc, *,
                 chunks, n_ctx_blocks, blocks_per_seq):
    i = pl.program_id(0)
    _stage_weights(i, w1f_ref, w2f_ref, w1_sc, w2_sc)

    @pl.when(i >= FFN_WSTEPS)
    def _():
        mod = _mod_rows(mod_ref, i - FFN_WSTEPS, n_ctx_blocks, blocks_per_seq)
        y = jnp.concatenate(
            [_rms(_swiglu_residual(x_ref[rows, :], mod, g_ref[2:3], w1_sc, w2_sc, chunks, 6),
                  fin_ref[...]) for rows in _row_parts(x_ref.shape[0])], axis=0)

        @pl.when(i - FFN_WSTEPS < n_ctx_blocks)
        def _():
            yp_ref[...] = y

        @pl.when(i - FFN_WSTEPS >= n_ctx_blocks)
        def _():
            ys_ref[...] = y


def _ff_chunks(d_ff):
    tile = MXU_TILE
    half = max(tile, (d_ff // 2 + tile - 1) // tile * tile)
    return ((0, half), (half, d_ff)) if half < d_ff else ((0, d_ff),)


def _inproj_kernel(h_ref, wa_ref, wb_ref, b_ref, z_ref, w_sc, *, n_lead, shift):
    j, i = pl.program_id(0), pl.program_id(1)

    @pl.when(jnp.logical_and(i == 0, j < n_lead))
    def _():
        w_sc[...] = wa_ref[...].astype(BF16)

    @pl.when(jnp.logical_and(i == 0, j >= n_lead))
    def _():
        w_sc[...] = jnp.concatenate([wa_ref[shift:, :], wb_ref[:shift, :]], axis=0).astype(BF16)

    def project():
        return lax.dot_general(h_ref[...], w_sc[...], (((1,), (1,)), ((), ())),
                               preferred_element_type=F32) + b_ref[...]

    is_sig = functools.reduce(jnp.logical_or, [j == g for g in SIG_GROUPS])

    @pl.when(is_sig)
    def _():
        z_ref[...] = (0.5 * jnp.tanh(0.5 * project()) + 0.5).astype(BF16)

    @pl.when(jnp.logical_not(is_sig))
    def _():
        z_ref[...] = project().astype(BF16)


def _kproj_kernel(h_ref, wk_ref, wg_ref, b_ref, kt_ref, gt_ref, wt_sc, *, k_scale):
    dm = kt_ref.shape[1]

    @pl.when(pl.program_id(0) == 0)
    def _():
        wt_sc[:dm, :] = (wk_ref[...] * k_scale).astype(BF16)
        wt_sc[dm:, :] = wg_ref[...].astype(BF16)

    full = lax.dot_general(wt_sc[...], h_ref[...], (((1,), (1,)), ((), ())),
                           preferred_element_type=F32) + b_ref[...]
    for c in range(kt_ref.shape[0]):
        kt_ref[c] = full[:dm, c * CHUNK:(c + 1) * CHUNK].astype(BF16)
    gt_ref[...] = full[dm:, :]


def _lane_scan(x, op, fill, reverse):
    n = x.shape[1]
    lane = lax.broadcasted_iota(jnp.int32, x.shape, 1)
    sh = 1
    while sh < n:
        if reverse:
            nb = jnp.where(lane < n - sh, pltpu.roll(x, n - sh, axis=1), fill)
        else:
            nb = jnp.where(lane >= sh, pltpu.roll(x, sh, axis=1), fill)
        x = op(x, nb)
        sh *= 2
    return x


def _mlstm_kernel(*refs, nc, n_sub, has_init, write_state, unroll):
    it = iter(refs)
    q_ref, kt_ref, v_ref, gr_ref, mln_ref = (next(it) for _ in range(5))
    if has_init:
        c0_ref, n0_ref, m0_ref = next(it), next(it), next(it)
    hn_ref = next(it)
    if write_state:
        cout_ref, nout_ref, mout_ref = next(it), next(it), next(it)
    c_sc, u_sc, wi_sc, cl_sc, wk_sc, vr_sc, dc_sc, mi_sc, mf_sc = (next(it) for _ in range(9))
    if nc > 1:
        h_sc = next(it)

    def head_norm(h):
        mean_sq = _bdot((h * h).astype(BF16), jnp.full((h.shape[1],) * 2, 1.0 / h.shape[1], BF16))
        return (h * lax.rsqrt(mean_sq + EPS) * mln_ref[...]).astype(BF16)

    L = CHUNK
    dh = q_ref.shape[1]
    rows = u_sc.shape[1]
    blk, h_idx = pl.program_id(0), pl.program_id(1)
    row = lax.broadcasted_iota(jnp.int32, (L, L), 0)
    col = lax.broadcasted_iota(jnp.int32, (L, L), 1)
    visible = (col <= row, col >= row)
    no_inbound = (not has_init) and nc == 1
    ones_ext = jnp.ones((L, LANES), BF16)

    def gate_rows(gate, d, hh):
        g_row = (gate * N_DIR + d) * NH + hh
        tokens = gr_ref[g_row:g_row + 1, :]
        return jnp.concatenate([tokens[:, r * L:(r + 1) * L] for r in range(rows)], axis=0)

    def gate_quantities(d, hh):
        rev = d == 1
        k = d * NH + hh
        ig = gate_rows(0, d, hh)
        lf = jax.nn.log_sigmoid(gate_rows(1, d, hh))
        b = _lane_scan(lf, jnp.add, 0.0, rev)
        tot = jnp.broadcast_to(b[:, 0:1] if rev else b[:, L - 1:L], (rows, L))
        vrow = ig - b
        cm = _lane_scan(vrow, jnp.maximum, -jnp.inf, rev)
        g = tot - b + ig
        gmax = jnp.broadcast_to(jnp.max(g, axis=1, keepdims=True), (rows, L))
        mi_sc[k] = jnp.zeros((rows, L), F32)
        mf_sc[k] = jnp.zeros(mf_sc.shape[1:], F32)
        for jj in range(n_sub):
            if has_init:
                m = jnp.full((1, L), m0_ref[((blk * n_sub + jj) * N_DIR + d) * NH + hh], F32)
            else:
                m = jnp.zeros((1, L), F32)
            for c in (range(nc - 1, -1, -1) if rev else range(nc)):
                r = jj * nc + c
                mi_sc[k, r:r + 1, :] = m
                m = jnp.maximum(tot[r:r + 1] + m, gmax[r:r + 1])
            mf_sc[k, jj:jj + 1, :] = m
        m_in = mi_sc[k]
        m_out = jnp.maximum(tot + m_in, gmax)
        mm = jnp.maximum(m_in, cm)
        u_sc[k] = -mm
        wi_sc[k] = jnp.exp(m_in - mm)
        cl_sc[k] = jnp.exp(-(b + mm))
        wk_sc[k] = jnp.exp(g - m_out)
        vr_sc[k] = vrow
        dc_sc[k] = jnp.exp(tot + m_in - m_out)

    @pl.when(h_idx == 0)
    def _():
        for d in range(N_DIR):
            for hh in range(NH):
                gate_quantities(d, hh)

    def chain(d, r, r0, s_raw):
        one = pl.ds(r, 1)
        k = d * NH + h_idx
        qc = q_ref[pl.ds(r0, L), :]
        ktc = kt_ref[r]
        v_ext = jnp.concatenate([v_ref[pl.ds(r0, L), :], ones_ext], axis=1)
        if s_raw is None:
            s_raw = _bdot(qc, ktc)
        def per_query(row_ref, width):
            col = jnp.broadcast_to(row_ref[k, one, :], (LANES, L)).T
            return jnp.concatenate([col] * (width // LANES), axis=1)

        s = s_raw * jnp.exp(jnp.where(visible[d], per_query(u_sc, L) + vr_sc[k, one, :], -jnp.inf))
        num = _bdot(s.astype(BF16), v_ext[:, :dh])
        den = jnp.sum(s[:, :LANES] + s[:, LANES:], axis=1, keepdims=True)
        if not no_inbound:
            inter = per_query(wi_sc, dh + LANES) * _bdot(qc, c_sc[d].astype(BF16))
            num = num + inter[:, :dh]
            den = den + inter[:, dh:]
        rden = 1.0 / jnp.maximum(jnp.abs(den), per_query(cl_sc, LANES))
        h = num * jnp.concatenate([rden] * (dh // LANES), axis=1)
        kw = (ktc.astype(F32) * wk_sc[k, one, :]).astype(BF16)
        kv = _bdot(kw, v_ext)
        if no_inbound:
            c_sc[d] = kv
        else:
            c_sc[d] = dc_sc[k, one, :][:, 0:1] * c_sc[d] + kv
        return h

    def do_seq(j):
        base = j * (nc * L)
        for d in range(N_DIR):
            if has_init:
                n_rep = jnp.broadcast_to(n0_ref[j, 0, d, pl.ds(h_idx, 1), :], (LANES, dh)).T
                c_sc[d] = jnp.concatenate([c0_ref[j, 0, d, 0], n_rep], axis=1)
            elif not no_inbound:
                c_sc[d] = jnp.zeros(c_sc.shape[1:], F32)

        if nc == 1:
            r0 = pl.multiple_of(base, L)
            s_raw = _bdot(q_ref[pl.ds(r0, L), :], kt_ref[j])
            hn_ref[pl.ds(r0, L), :] = head_norm(chain(0, j, r0, s_raw) + chain(1, j, r0, s_raw))
        else:
            def step(i, first_visit):
                for d in range(N_DIR):
                    c = i if d == 0 else nc - 1 - i
                    r0 = pl.multiple_of(base + c * L, L)
                    h = chain(d, j * nc + c, r0, None)
                    if first_visit:
                        h_sc[pl.ds(r0, L), :] = h
                    else:
                        hn_ref[pl.ds(r0, L), :] = head_norm(h_sc[pl.ds(r0, L), :] + h)

            lax.fori_loop(0, nc // 2, lambda i, carry: (step(i, True), carry)[1], 0,
                          unroll=unroll)
            lax.fori_loop(nc // 2, nc, lambda i, carry: (step(i, False), carry)[1], 0,
                          unroll=unroll)

        if write_state:
            for d in range(N_DIR):
                cout_ref[j, 0, d, 0] = c_sc[d, :, :dh]
                nout_ref[j, 0, d, pl.ds(h_idx, 1), :] = c_sc[d, :, dh:].T[0:1, :]
                mout_ref[j, d, 0] = mf_sc[d * NH + h_idx, pl.ds(j, 1), :][:, :LANES]

    if n_sub == 1:
        do_seq(0)
    else:
        lax.fori_loop(0, n_sub, lambda j, carry: (do_seq(j), carry)[1], 0, unroll=unroll)


def _mlstm_call(z, kt, gates_t, ml_norm, state, *, n_seq, t_len, n_sub, tok_block0, dh, unroll):
    nc = t_len // CHUNK
    rows = n_sub * nc
    assert rows % SUBLANES == 0 and (nc == 1 or nc % 2 == 0)
    has_init = state is not None
    write_state = not has_init
    kern = functools.partial(_mlstm_kernel, nc=nc, n_sub=n_sub, has_init=has_init,
                             write_state=write_state, unroll=unroll)
    tb = n_sub * t_len

    def z_spec(group):
        return pl.BlockSpec((tb, dh), lambda b, h: (tok_block0 + b, group * NH + h))

    in_specs = [z_spec(ZQ),
                pl.BlockSpec((rows, dh, CHUNK), lambda b, h: (tok_block0 + b, h, 0)),
                z_spec(ZV),
                pl.BlockSpec((gates_t.shape[0], tb), lambda b, h: (0, tok_block0 + b)),
                pl.BlockSpec((1, dh), lambda b, h: (0, h))]
    args = [z, kt, z, gates_t, ml_norm]
    if has_init:
        c0, n0, m0 = state
        in_specs += [pl.BlockSpec((n_sub, 1, N_DIR, 1, dh, dh), lambda b, h: (b, 0, 0, h, 0, 0)),
                     pl.BlockSpec((n_sub, 1, N_DIR, NH, dh), lambda b, h: (b, 0, 0, 0, 0)),
                     pl.BlockSpec(memory_space=pltpu.SMEM)]
        args += [c0, n0, m0.reshape(-1)]
    out_specs = [pl.BlockSpec((tb, dh), lambda b, h: (b, h))]
    out_shape = [jax.ShapeDtypeStruct((n_seq * t_len, NH * dh), BF16)]
    if write_state:
        out_specs += [pl.BlockSpec((n_sub, 1, N_DIR, 1, dh, dh), lambda b, h: (b, 0, 0, h, 0, 0)),
                      pl.BlockSpec((n_sub, 1, N_DIR, NH, dh), lambda b, h: (b, 0, 0, 0, 0)),
                      pl.BlockSpec((n_sub, N_DIR, 1, 1, LANES), lambda b, h: (b, 0, h, 0, 0))]
        out_shape += [jax.ShapeDtypeStruct((n_seq, 1, N_DIR, NH, dh, dh), F32),
                      jax.ShapeDtypeStruct((n_seq, 1, N_DIR, NH, dh), F32),
                      jax.ShapeDtypeStruct((n_seq, N_DIR, NH, 1, LANES), F32)]
    per_row = pltpu.VMEM((N_DIR * NH, rows, CHUNK), F32)
    return pl.pallas_call(
        kern,
        grid=(n_seq // n_sub, NH),
        in_specs=in_specs,
        out_specs=out_specs,
        out_shape=out_shape,
        scratch_shapes=[pltpu.VMEM((N_DIR, dh, dh + LANES), F32)] + [per_row] * 7
        + [pltpu.VMEM((N_DIR * NH, max(SUBLANES, n_sub), CHUNK), F32)]
        + ([pltpu.VMEM((tb, dh), F32)] if nc > 1 else []),
        compiler_params=_params(2),
        name="mlstm_lat" if has_init else "mlstm_ctx",
    )(*args)


def _mixout_kernel(hnc_ref, hnl_ref, zo_ref, zb_ref, zc_ref, zx_ref, zgm_ref, zgs_ref, x1_ref,
                   mod_ref, cw_ref, cb_ref, wml32_ref, wsc32_ref, wo32_ref, x2_ref,
                   wml_ref, wsc_ref, wo_ref, *, n_ctx_blocks, blocks_per_seq, ctx_seg):
    @pl.when(pl.program_id(0) == 0)
    def _():
        wml_ref[...] = wml32_ref[...].astype(BF16)
        wsc_ref[...] = wsc32_ref[...].astype(BF16)
        wo_ref[...] = wo32_ref[...].astype(BF16)

    is_ctx = pl.program_id(0) < n_ctx_blocks
    gate = _mod_rows(mod_ref, pl.program_id(0), n_ctx_blocks, blocks_per_seq)[5]
    sub = MIX_SUB
    t = lax.broadcasted_iota(jnp.int32, (sub, 1), 0)
    pos = jnp.where(is_ctx, t % ctx_seg, t % GRID_W)
    has_prev = pos != 0
    has_next = pos != jnp.where(is_ctx, ctx_seg - 1, GRID_W - 1)
    for r in range(0, x1_ref.shape[0], sub):
        rows = pl.ds(r, sub)
        hn = jnp.where(is_ctx, hnc_ref[rows, :], hnl_ref[rows, :])
        ml = _bdot(zo_ref[rows, :] * hn, wml_ref[...])

        u = zc_ref[rows, :].astype(F32) * zx_ref[rows, :].astype(F32)
        u_prev = jnp.where(has_prev, pltpu.roll(u, 1, axis=0), 0.0)
        u_next = jnp.where(has_next, pltpu.roll(u, sub - 1, axis=0), 0.0)
        uc = u_prev * cw_ref[0:1] + u * cw_ref[1:2] + u_next * cw_ref[2:3] + cb_ref[...]
        sc = _bdot(zb_ref[rows, :] * uc.astype(BF16), wsc_ref[...])

        y = zgm_ref[rows, :].astype(F32) * ml + zgs_ref[rows, :].astype(F32) * sc
        x2_ref[rows, :] = x1_ref[rows, :] + gate * _bdot(y.astype(BF16), wo_ref[...])


def kernel(x_prompt, x_sample, state_C, state_n, state_m, c, c_ctx, w_mod, b_mod, norm_g,
           ffn1_w1, ffn1_w2, w_in, b_in, conv_w, conv_b, ml_norm, w_ml_out, w_sc_out, w_o,
           ffn2_w1, ffn2_w2, final_norm):
    assert w_mod.shape[0] == 1, "single trunk layer only"
    bp, tp, dm = x_prompt.shape
    bs, ts, _ = x_sample.shape
    n_ctx, n_lat = bp * tp, bs * ts
    n_tok = n_ctx + n_lat
    d_ff = ffn1_w2.shape[1]
    ml_dim = w_ml_out.shape[1]
    sc_dim = w_sc_out.shape[1]
    dh = ml_dim // NH
    n_gate = 2 * N_DIR * NH
    assert ml_dim == dm and sc_dim == dm and dh % MXU_TILE == 0 and dm == TN_IN
    assert w_in.shape[2] == 9 * dm + n_gate and n_gate < LANES
    assert n_ctx % TM_FFN == 0 and ts % TM_FFN == 0 and n_tok % TM_IN == 0
    assert n_ctx % TM_MIX == 0 and ts % TM_MIX == 0 and MIX_SUB % tp == 0 and MIX_SUB % GRID_W == 0
    assert TM_K % CHUNK == 0 and n_tok % TM_K == 0
    assert tp == CHUNK and ts % (2 * CHUNK) == 0 and bp % CTX_SEQS == 0 and n_ctx % ts == 0
    n_cond = 1 + bs

    cond = jnp.concatenate([c_ctx[None, :], c], axis=0)
    condT = jnp.pad(cond.T, ((0, 0), (0, SUBLANES - n_cond)))
    mod_spec = _resident((N_MOD, SUBLANES, dm))

    assert w_mod.shape[2] == N_MOD * dm and N_MOD <= FFN_WSTEPS
    w_in0, b_in0 = w_in[0], b_in.reshape(-1)
    gate_lo = 4 * ml_dim
    ncb = n_ctx // TM_FFN
    nb = n_tok // TM_FFN
    chunks = _ff_chunks(d_ff)
    wc1, wc2 = 2 * d_ff // FFN_WSTEPS, d_ff // FFN_WSTEPS
    assert wc1 * FFN_WSTEPS == 2 * d_ff and wc1 % LANES == 0
    assert wc2 * FFN_WSTEPS == d_ff and wc2 % (2 * SUBLANES) == 0

    def tok(i):
        return jnp.maximum(i - FFN_WSTEPS, 0)

    def wstep(i):
        return jnp.minimum(i, FFN_WSTEPS - 1)

    ffn_w_specs = [pl.BlockSpec((dm, wc1), lambda i: (0, wstep(i))),
                   pl.BlockSpec((wc2, dm), lambda i: (wstep(i), 0))]
    ffn_w_scratch = [pltpu.VMEM((dm, 2 * d_ff), BF16), pltpu.VMEM((d_ff, dm), BF16)]
    tok_spec = pl.BlockSpec((TM_FFN, dm), lambda i: (tok(i), 0))
    x1, h2, mod = pl.pallas_call(
        functools.partial(_ffn1_kernel, chunks=chunks, n_ctx_blocks=ncb,
                          blocks_per_seq=ts // TM_FFN, n_cond=n_cond),
        grid=(FFN_WSTEPS + nb,),
        in_specs=[pl.BlockSpec((TM_FFN, dm), lambda i: (jnp.minimum(tok(i), ncb - 1), 0)),
                  pl.BlockSpec((TM_FFN, dm), lambda i: (jnp.maximum(tok(i) - ncb, 0), 0)),
                  _resident((dm, SUBLANES)),
                  pl.BlockSpec((dm, dm), lambda i: (0, jnp.minimum(i, N_MOD - 1))),
                  _resident((N_MOD, dm)), _resident((3, dm))] + ffn_w_specs,
        out_specs=[tok_spec, tok_spec,
                   pl.BlockSpec((N_MOD, SUBLANES, dm), lambda i: (0, 0, 0))],
        out_shape=[jax.ShapeDtypeStruct((n_tok, dm), F32), jax.ShapeDtypeStruct((n_tok, dm), BF16),
                   jax.ShapeDtypeStruct((N_MOD, SUBLANES, dm), F32)],
        scratch_shapes=ffn_w_scratch,
        compiler_params=_params(1),
        name="ffn1",
    )(x_prompt.reshape(n_ctx, dm), x_sample.reshape(n_lat, dm), condT, w_mod[0],
      b_mod.reshape(N_MOD, dm), norm_g[0], ffn1_w1[0], ffn1_w2[0])

    w_in_t = w_in0.T
    n_lead = gate_lo // TN_IN - 1
    b_main = jnp.concatenate([b_in0[:ml_dim], b_in0[2 * ml_dim:gate_lo],
                              b_in0[gate_lo + n_gate:]])[None, :]
    n_main = b_main.shape[1]

    def w_block(j):
        return jnp.where(j == 0, 0, j + 1)

    z = pl.pallas_call(
        functools.partial(_inproj_kernel, n_lead=n_lead, shift=n_gate),
        grid=(n_main // TN_IN, n_tok // TM_IN),
        in_specs=[pl.BlockSpec((TM_IN, dm), lambda j, i: (i, 0)),
                  pl.BlockSpec((TN_IN, dm), lambda j, i: (w_block(j), 0)),
                  pl.BlockSpec((TN_IN, dm), lambda j, i: (w_block(j) + 1, 0)),
                  pl.BlockSpec((1, TN_IN), lambda j, i: (0, j))],
        out_specs=pl.BlockSpec((TM_IN, TN_IN), lambda j, i: (i, j)),
        out_shape=jax.ShapeDtypeStruct((n_tok, n_main), BF16),
        scratch_shapes=[pltpu.VMEM((TN_IN, dm), BF16)],
        compiler_params=_params(2),
        name="inproj",
    )(h2, w_in_t, w_in_t, b_main)

    kscale = dh ** -0.5
    assert ml_dim % dm == 0 and gate_lo % n_gate == 0
    b_kg = jnp.concatenate([b_in0[ml_dim:2 * ml_dim] * kscale,
                            b_in0[gate_lo:gate_lo + n_gate]])[:, None]
    kt, gates_t = pl.pallas_call(
        functools.partial(_kproj_kernel, k_scale=kscale),
        grid=(n_tok // TM_K,),
        in_specs=[pl.BlockSpec((TM_K, dm), lambda i: (i, 0)),
                  pl.BlockSpec((dm, dm), lambda i: (ml_dim // dm, 0),
                               pipeline_mode=pl.Buffered(1)),
                  pl.BlockSpec((n_gate, dm), lambda i: (gate_lo // n_gate, 0),
                               pipeline_mode=pl.Buffered(1)),
                  _resident((dm + n_gate, 1))],
        out_specs=[pl.BlockSpec((TM_K // CHUNK, dm, CHUNK), lambda i: (i, 0, 0)),
                   pl.BlockSpec((n_gate, TM_K), lambda i: (0, i))],
        out_shape=[jax.ShapeDtypeStruct((n_tok // CHUNK, dm, CHUNK), BF16),
                   jax.ShapeDtypeStruct((n_gate, n_tok), F32)],
        scratch_shapes=[pltpu.VMEM((dm + n_gate, dm), BF16)],
        compiler_params=_params(1),
        name="kproj",
    )(h2, w_in_t, w_in_t, b_kg)

    hn_ctx, new_c, new_n, new_m = _mlstm_call(
        z, kt, gates_t, ml_norm, None,
        n_seq=bp, t_len=tp, n_sub=CTX_SEQS, tok_block0=0, dh=dh, unroll=4)
    (hn_lat,) = _mlstm_call(
        z, kt, gates_t, ml_norm, (state_C, state_n, state_m),
        n_seq=bs, t_len=ts, n_sub=1, tok_block0=n_ctx // ts, dh=dh, unroll=4)

    ncb_m = n_ctx // TM_MIX
    tokm = pl.BlockSpec((TM_MIX, dm), lambda i: (i, 0))

    def zcol(j):
        return pl.BlockSpec((TM_MIX, dm), lambda i: (i, j))

    x2 = pl.pallas_call(
        functools.partial(_mixout_kernel, n_ctx_blocks=ncb_m, blocks_per_seq=ts // TM_MIX,
                          ctx_seg=tp),
        grid=(n_tok // TM_MIX,),
        in_specs=[pl.BlockSpec((TM_MIX, dm), lambda i: (jnp.minimum(i, ncb_m - 1), 0)),
                  pl.BlockSpec((TM_MIX, dm), lambda i: (jnp.maximum(i - ncb_m, 0), 0)),
                  zcol(ZO), zcol(ZB), zcol(ZC), zcol(ZX), zcol(ZGM), zcol(ZGS), tokm,
                  mod_spec, _resident((3, dm)), _resident((1, dm)),
                  _resident((dm, dm)), _resident((dm, dm)), _resident((dm, dm))],
        out_specs=tokm,
        out_shape=jax.ShapeDtypeStruct((n_tok, dm), F32),
        scratch_shapes=[pltpu.VMEM((dm, dm), BF16)] * 3,
        compiler_params=_params(1),
        name="mixout",
    )(hn_ctx, hn_lat, z, z, z, z, z, z, x1, mod, conv_w[0],
      conv_b[0][None, :], w_ml_out[0], w_sc_out[0], w_o[0])

    y_p, y_s = pl.pallas_call(
        functools.partial(_ffn2_kernel, chunks=chunks, n_ctx_blocks=ncb,
                          blocks_per_seq=ts // TM_FFN),
        grid=(FFN_WSTEPS + nb,),
        in_specs=[tok_spec, mod_spec, _resident((3, dm)), _resident((1, dm))]
        + ffn_w_specs,
        out_specs=[pl.BlockSpec((TM_FFN, dm), lambda i: (jnp.minimum(tok(i), ncb - 1), 0)),
                   pl.BlockSpec((TM_FFN, dm), lambda i: (jnp.maximum(tok(i) - ncb, 0), 0))],
        out_shape=[jax.ShapeDtypeStruct((n_ctx, dm), F32), jax.ShapeDtypeStruct((n_lat, dm), F32)],
        scratch_shapes=ffn_w_scratch,
        compiler_params=_params(1),
        name="ffn2",
    )(x2, mod, norm_g[0], final_norm[None, :], ffn2_w1[0], ffn2_w2[0])

    return (y_p.reshape(bp, tp, dm), y_s.reshape(bs, ts, dm), new_c,
            new_n, new_m[..., 0, 0].reshape(bp, 1, N_DIR, NH))
```

```python
import functools

import jax
import jax.numpy as jnp
from jax import lax
from jax.experimental import pallas as pl
from jax.experimental.pallas import tpu as pltpu

F32 = jnp.float32
BF16 = jnp.bfloat16

NH = 4
N_DIR = 2
N_MOD = 9
GRID_W = 64
EPS = 1e-6
LANES = 128
SUBLANES = 8
MXU_TILE = 256
VMEM_LIMIT = 56 * 1024 * 1024

TM_FFN = 512
FFN_SUB = 256
FFN_WSTEPS = 11
TM_MIX = 512
MIX_SUB = 256
TM_IN = 2048
TN_IN = 1024
TM_K = 1024
CHUNK = 256
CTX_SEQS = 8
ZQ, ZV, ZO, ZB, ZC, ZX, ZGM, ZGS = range(8)
SIG_GROUPS = (ZO, ZGM, ZGS)


def _params(n_axes):
    return pltpu.CompilerParams(dimension_semantics=("arbitrary",) * n_axes,
                                vmem_limit_bytes=VMEM_LIMIT)


def _resident(shape):
    zeros = (0,) * len(shape)
    return pl.BlockSpec(shape, lambda *_: zeros, pipeline_mode=pl.Buffered(1))


def _rms(x, g):
    return x * lax.rsqrt(jnp.mean(x * x, axis=-1, keepdims=True) + EPS) * g


def _bdot(a, b):
    return jnp.dot(a, b, preferred_element_type=F32)


def _mod_rows(mod_ref, t, n_ctx_blocks, blocks_per_seq):
    r = jnp.where(t < n_ctx_blocks, 0, 1 + (t - n_ctx_blocks) // blocks_per_seq)
    return [mod_ref[k, pl.ds(r, 1), :] for k in range(N_MOD)]


def _swiglu_residual(x, mod, g_norm, w1_ref, w2_ref, chunks, row0):
    d_ff = w2_ref.shape[0]
    h = (_rms(x, g_norm) * (1.0 + mod[row0 + 1]) + mod[row0]).astype(BF16)
    acc = None
    for lo, hi in chunks:
        a = _bdot(h, w1_ref[:, lo:hi])
        b = _bdot(h, w1_ref[:, d_ff + lo:d_ff + hi])
        half_a = 0.5 * a
        part = _bdot(((half_a + half_a * jnp.tanh(half_a)) * b).astype(BF16), w2_ref[lo:hi, :])
        acc = part if acc is None else acc + part
    return x + (0.5 * mod[row0 + 2]) * acc


def _row_parts(tm):
    return [pl.ds(r, FFN_SUB) for r in range(0, tm, FFN_SUB)]


def _stage_weights(i, w1f_ref, w2f_ref, w1_sc, w2_sc, also=None):
    c1, c2 = w1f_ref.shape[1], w2f_ref.shape[0]
    for c in range(FFN_WSTEPS):
        @pl.when(i == c)
        def _(c=c):
            w1_sc[:, c * c1:(c + 1) * c1] = w1f_ref[...].astype(BF16)
            w2_sc[c * c2:(c + 1) * c2, :] = w2f_ref[...].astype(BF16)
            if also is not None:
                also(c)


def _ffn1_kernel(xp_ref, xs_ref, condT_ref, wmod_ref, bmod_ref, g_ref, w1f_ref, w2f_ref,
                 x1_ref, h2_ref, mod_ref, w1_sc, w2_sc, *,
                 chunks, n_ctx_blocks, blocks_per_seq, n_cond):
    i = pl.program_id(0)

    def mod_chunk(c):
        if c < N_MOD:
            s = jax.nn.silu(condT_ref[...])
            w = wmod_ref[...]
            rows = [jnp.sum(w * s[:, r:r + 1], axis=0, keepdims=True) for r in range(n_cond)]
            rows.append(jnp.zeros((SUBLANES - n_cond, w.shape[1]), F32))
            mod_ref[c] = jnp.concatenate(rows, axis=0) + bmod_ref[c:c + 1, :]

    _stage_weights(i, w1f_ref, w2f_ref, w1_sc, w2_sc, also=mod_chunk)

    @pl.when(i >= FFN_WSTEPS)
    def _():
        t = i - FFN_WSTEPS
        is_ctx = t < n_ctx_blocks
        mod = _mod_rows(mod_ref, t, n_ctx_blocks, blocks_per_seq)
        for rows in _row_parts(x1_ref.shape[0]):
            x = jnp.where(is_ctx, xp_ref[rows, :], xs_ref[rows, :])
            x1 = _swiglu_residual(x, mod, g_ref[0:1], w1_sc, w2_sc, chunks, 0)
            x1_ref[rows, :] = x1
            h2_ref[rows, :] = (_rms(x1, g_ref[1:2]) * (1.0 + mod[4]) + mod[3]).astype(BF16)


def _ffn2_kernel(x_ref, mod_ref, g_ref, fin_ref, w1f_ref, w2f_ref, yp_ref, ys_ref, w1_sc, w2_sc, *,
                 chunks, n_ctx_blocks, blocks_per_seq):
    i = pl.program_id(0)
    _stage_weights(i, w1f_ref, w2f_ref, w1_sc, w2_sc)

    @pl.when(i >= FFN_WSTEPS)
    def _():
        mod = _mod_rows(mod_ref, i - FFN_WSTEPS, n_ctx_blocks, blocks_per_seq)
        y = jnp.concatenate(
            [_rms(_swiglu_residual(x_ref[rows, :], mod, g_ref[2:3], w1_sc, w2_sc, chunks, 6),
                  fin_ref[...]) for rows in _row_parts(x_ref.shape[0])], axis=0)

        @pl.when(i - FFN_WSTEPS < n_ctx_blocks)
        def _():
            yp_ref[...] = y

        @pl.when(i - FFN_WSTEPS >= n_ctx_blocks)
        def _():
            ys_ref[...] = y


def _ff_chunks(d_ff):
    tile = MXU_TILE
    half = max(tile, (d_ff // 2 + tile - 1) // tile * tile)
    return ((0, half), (half, d_ff)) if half < d_ff else ((0, d_ff),)


def _inproj_kernel(h_ref, wa_ref, wb_ref, b_ref, z_ref, w_sc, *, n_lead, shift):
    j, i = pl.program_id(0), pl.program_id(1)
    is_sig = functools.reduce(jnp.logical_or, [j == g for g in SIG_GROUPS])
    pre = jnp.where(is_sig, 0.5, 1.0)

    @pl.when(jnp.logical_and(i == 0, j < n_lead))
    def _():
        w_sc[...] = (wa_ref[...] * pre).astype(BF16)

    @pl.when(jnp.logical_and(i == 0, j >= n_lead))
    def _():
        w_sc[...] = (jnp.concatenate([wa_ref[shift:, :], wb_ref[:shift, :]], axis=0)
                     * pre).astype(BF16)

    def project():
        return lax.dot_general(h_ref[...], w_sc[...], (((1,), (1,)), ((), ())),
                               preferred_element_type=F32) + b_ref[...] * pre

    @pl.when(is_sig)
    def _():
        z_ref[...] = jnp.tanh(project()).astype(BF16)

    @pl.when(jnp.logical_not(is_sig))
    def _():
        z_ref[...] = project().astype(BF16)


def _kproj_kernel(h_ref, wk_ref, wg_ref, b_ref, kt_ref, gt_ref, wt_sc, *, k_scale):
    dm = kt_ref.shape[1]

    @pl.when(pl.program_id(0) == 0)
    def _():
        wt_sc[:dm, :] = (wk_ref[...] * k_scale).astype(BF16)
        wt_sc[dm:, :] = wg_ref[...].astype(BF16)

    full = lax.dot_general(wt_sc[...], h_ref[...], (((1,), (1,)), ((), ())),
                           preferred_element_type=F32) + b_ref[...]
    for c in range(kt_ref.shape[0]):
        kt_ref[c] = full[:dm, c * CHUNK:(c + 1) * CHUNK].astype(BF16)
    gt_ref[...] = full[dm:, :]


def _lane_scan(x, op, fill, reverse):
    n = x.shape[1]
    lane = lax.broadcasted_iota(jnp.int32, x.shape, 1)
    sh = 1
    while sh < n:
        if reverse:
            nb = jnp.where(lane < n - sh, pltpu.roll(x, n - sh, axis=1), fill)
        else:
            nb = jnp.where(lane >= sh, pltpu.roll(x, sh, axis=1), fill)
        x = op(x, nb)
        sh *= 2
    return x


def _mlstm_kernel(*refs, nc, n_sub, has_init, write_state, unroll):
    it = iter(refs)
    q_ref, kt_ref, v_ref, gr_ref, mln_ref = (next(it) for _ in range(5))
    if has_init:
        c0_ref, n0_ref, m0_ref = next(it), next(it), next(it)
    hn_ref = next(it)
    if write_state:
        cout_ref, nout_ref, mout_ref = next(it), next(it), next(it)
    c_sc, u_sc, wi_sc, cl_sc, wk_sc, vr_sc, dc_sc, mi_sc, mf_sc = (next(it) for _ in range(9))
    if nc > 1:
        h_sc = next(it)

    def head_norm(h):
        mean_sq = _bdot((h * h).astype(BF16), jnp.full((h.shape[1],) * 2, 1.0 / h.shape[1], BF16))
        return (h * lax.rsqrt(mean_sq + EPS) * mln_ref[...]).astype(BF16)

    L = CHUNK
    dh = q_ref.shape[1]
    rows = u_sc.shape[1]
    blk, h_idx = pl.program_id(0), pl.program_id(1)
    row = lax.broadcasted_iota(jnp.int32, (L, L), 0)
    col = lax.broadcasted_iota(jnp.int32, (L, L), 1)
    visible = (col <= row, col >= row)
    no_inbound = (not has_init) and nc == 1
    ones_ext = jnp.ones((L, LANES), BF16)

    def gate_rows(gate, d, hh):
        g_row = (gate * N_DIR + d) * NH + hh
        tokens = gr_ref[g_row:g_row + 1, :]
        return jnp.concatenate([tokens[:, r * L:(r + 1) * L] for r in range(rows)], axis=0)

    def gate_quantities(d, hh):
        rev = d == 1
        k = d * NH + hh
        ig = gate_rows(0, d, hh)
        lf = jax.nn.log_sigmoid(gate_rows(1, d, hh))
        b = _lane_scan(lf, jnp.add, 0.0, rev)
        tot = jnp.broadcast_to(b[:, 0:1] if rev else b[:, L - 1:L], (rows, L))
        vrow = ig - b
        cm = _lane_scan(vrow, jnp.maximum, -jnp.inf, rev)
        g = tot - b + ig
        gmax = jnp.broadcast_to(jnp.max(g, axis=1, keepdims=True), (rows, L))
        mi_sc[k] = jnp.zeros((rows, L), F32)
        mf_sc[k] = jnp.zeros(mf_sc.shape[1:], F32)
        for jj in range(n_sub):
            if has_init:
                m = jnp.full((1, L), m0_ref[((blk * n_sub + jj) * N_DIR + d) * NH + hh], F32)
            else:
                m = jnp.zeros((1, L), F32)
            for c in (range(nc - 1, -1, -1) if rev else range(nc)):
                r = jj * nc + c
                mi_sc[k, r:r + 1, :] = m
                m = jnp.maximum(tot[r:r + 1] + m, gmax[r:r + 1])
            mf_sc[k, jj:jj + 1, :] = m
        m_in = mi_sc[k]
        m_out = jnp.maximum(tot + m_in, gmax)
        mm = jnp.maximum(m_in, cm)
        u_sc[k] = -mm
        wi_sc[k] = jnp.exp(m_in - mm)
        cl_sc[k] = jnp.exp(-(b + mm))
        wk_sc[k] = jnp.exp(g - m_out)
        vr_sc[k] = vrow
        dc_sc[k] = jnp.exp(tot + m_in - m_out)

    @pl.when(h_idx == 0)
    def _():
        for d in range(N_DIR):
            for hh in range(NH):
                gate_quantities(d, hh)

    def chain(d, r, r0, s_raw):
        one = pl.ds(r, 1)
        k = d * NH + h_idx
        qc = q_ref[pl.ds(r0, L), :]
        ktc = kt_ref[r]
        v_ext = jnp.concatenate([v_ref[pl.ds(r0, L), :], ones_ext], axis=1)
        if s_raw is None:
            s_raw = _bdot(qc, ktc)
        def per_query(row_ref, width):
            col = jnp.broadcast_to(row_ref[k, one, :], (LANES, L)).T
            return jnp.concatenate([col] * (width // LANES), axis=1)

        s = s_raw * jnp.exp(jnp.where(visible[d], per_query(u_sc, L) + vr_sc[k, one, :], -jnp.inf))
        num = _bdot(s.astype(BF16), v_ext[:, :dh])
        den = jnp.sum(s[:, :LANES] + s[:, LANES:], axis=1, keepdims=True)
        if not no_inbound:
            inter = per_query(wi_sc, dh + LANES) * _bdot(qc, c_sc[d].astype(BF16))
            num = num + inter[:, :dh]
            den = den + inter[:, dh:]
        rden = 1.0 / jnp.maximum(jnp.abs(den), per_query(cl_sc, LANES))
        h = num * jnp.concatenate([rden] * (dh // LANES), axis=1)
        kw = (ktc.astype(F32) * wk_sc[k, one, :]).astype(BF16)
        kv = _bdot(kw, v_ext)
        if no_inbound:
            c_sc[d] = kv
        else:
            c_sc[d] = dc_sc[k, one, :][:, 0:1] * c_sc[d] + kv
        return h

    def do_seq(j):
        base = j * (nc * L)
        for d in range(N_DIR):
            if has_init:
                n_rep = jnp.broadcast_to(n0_ref[j, 0, d, pl.ds(h_idx, 1), :], (LANES, dh)).T
                c_sc[d] = jnp.concatenate([c0_ref[j, 0, d, 0], n_rep], axis=1)
            elif not no_inbound:
                c_sc[d] = jnp.zeros(c_sc.shape[1:], F32)

        if nc == 1:
            r0 = pl.multiple_of(base, L)
            s_raw = _bdot(q_ref[pl.ds(r0, L), :], kt_ref[j])
            hn_ref[pl.ds(r0, L), :] = head_norm(chain(0, j, r0, s_raw) + chain(1, j, r0, s_raw))
        else:
            def step(i, first_visit):
                for d in range(N_DIR):
                    c = i if d == 0 else nc - 1 - i
                    r0 = pl.multiple_of(base + c * L, L)
                    h = chain(d, j * nc + c, r0, None)
                    if first_visit:
                        h_sc[pl.ds(r0, L), :] = h
                    else:
                        hn_ref[pl.ds(r0, L), :] = head_norm(h_sc[pl.ds(r0, L), :] + h)

            lax.fori_loop(0, nc // 2, lambda i, carry: (step(i, True), carry)[1], 0,
                          unroll=unroll)
            lax.fori_loop(nc // 2, nc, lambda i, carry: (step(i, False), carry)[1], 0,
                          unroll=unroll)

        if write_state:
            for d in range(N_DIR):
                cout_ref[j, 0, d, 0] = c_sc[d, :, :dh]
                nout_ref[j, 0, d, pl.ds(h_idx, 1), :] = c_sc[d, :, dh:].T[0:1, :]
                mout_ref[j, d, 0] = mf_sc[d * NH + h_idx, pl.ds(j, 1), :][:, :LANES]

    if n_sub == 1:
        do_seq(0)
    else:
        lax.fori_loop(0, n_sub, lambda j, carry: (do_seq(j), carry)[1], 0, unroll=unroll)


def _mlstm_call(z, kt, gates_t, ml_norm, state, *, n_seq, t_len, n_sub, tok_block0, dh, unroll):
    nc = t_len // CHUNK
    rows = n_sub * nc
    assert rows % SUBLANES == 0 and (nc == 1 or nc % 2 == 0)
    has_init = state is not None
    write_state = not has_init
    kern = functools.partial(_mlstm_kernel, nc=nc, n_sub=n_sub, has_init=has_init,
                             write_state=write_state, unroll=unroll)
    tb = n_sub * t_len

    def z_spec(group):
        return pl.BlockSpec((tb, dh), lambda b, h: (tok_block0 + b, group * NH + h))

    in_specs = [z_spec(ZQ),
                pl.BlockSpec((rows, dh, CHUNK), lambda b, h: (tok_block0 + b, h, 0)),
                z_spec(ZV),
                pl.BlockSpec((gates_t.shape[0], tb), lambda b, h: (0, tok_block0 + b)),
                pl.BlockSpec((1, dh), lambda b, h: (0, h))]
    args = [z, kt, z, gates_t, ml_norm]
    if has_init:
        c0, n0, m0 = state
        in_specs += [pl.BlockSpec((n_sub, 1, N_DIR, 1, dh, dh), lambda b, h: (b, 0, 0, h, 0, 0)),
                     pl.BlockSpec((n_sub, 1, N_DIR, NH, dh), lambda b, h: (b, 0, 0, 0, 0)),
                     pl.BlockSpec(memory_space=pltpu.SMEM)]
        args += [c0, n0, m0.reshape(-1)]
    out_specs = [pl.BlockSpec((tb, dh), lambda b, h: (b, h))]
    out_shape = [jax.ShapeDtypeStruct((n_seq * t_len, NH * dh), BF16)]
    if write_state:
        out_specs += [pl.BlockSpec((n_sub, 1, N_DIR, 1, dh, dh), lambda b, h: (b, 0, 0, h, 0, 0)),
                      pl.BlockSpec((n_sub, 1, N_DIR, NH, dh), lambda b, h: (b, 0, 0, 0, 0)),
                      pl.BlockSpec((n_sub, N_DIR, 1, 1, LANES), lambda b, h: (b, 0, h, 0, 0))]
        out_shape += [jax.ShapeDtypeStruct((n_seq, 1, N_DIR, NH, dh, dh), F32),
                      jax.ShapeDtypeStruct((n_seq, 1, N_DIR, NH, dh), F32),
                      jax.ShapeDtypeStruct((n_seq, N_DIR, NH, 1, LANES), F32)]
    per_row = pltpu.VMEM((N_DIR * NH, rows, CHUNK), F32)
    return pl.pallas_call(
        kern,
        grid=(n_seq // n_sub, NH),
        in_specs=in_specs,
        out_specs=out_specs,
        out_shape=out_shape,
        scratch_shapes=[pltpu.VMEM((N_DIR, dh, dh + LANES), F32)] + [per_row] * 7
        + [pltpu.VMEM((N_DIR * NH, max(SUBLANES, n_sub), CHUNK), F32)]
        + ([pltpu.VMEM((tb, dh), F32)] if nc > 1 else []),
        compiler_params=_params(2),
        name="mlstm_lat" if has_init else "mlstm_ctx",
    )(*args)


def _mixout_kernel(hnc_ref, hnl_ref, zo_ref, zb_ref, zc_ref, zx_ref, zgm_ref, zgs_ref, x1_ref,
                   mod_ref, cw_ref, cb_ref, wml32_ref, wsc32_ref, wo32_ref, x2_ref,
                   wml_ref, wsc_ref, wo_ref, *, n_ctx_blocks, blocks_per_seq, ctx_seg):
    @pl.when(pl.program_id(0) == 0)
    def _():
        wml_ref[...] = (wml32_ref[...] * 0.5).astype(BF16)
        wsc_ref[...] = wsc32_ref[...].astype(BF16)
        wo_ref[...] = wo32_ref[...].astype(BF16)

    is_ctx = pl.program_id(0) < n_ctx_blocks
    gate = 0.5 * _mod_rows(mod_ref, pl.program_id(0), n_ctx_blocks, blocks_per_seq)[5]
    sub = MIX_SUB
    t = lax.broadcasted_iota(jnp.int32, (sub, 1), 0)
    pos = jnp.where(is_ctx, t % ctx_seg, t % GRID_W)
    has_prev = pos != 0
    has_next = pos != jnp.where(is_ctx, ctx_seg - 1, GRID_W - 1)
    for r in range(0, x1_ref.shape[0], sub):
        rows = pl.ds(r, sub)
        hn = jnp.where(is_ctx, hnc_ref[rows, :], hnl_ref[rows, :])
        ml = _bdot((zo_ref[rows, :] + 1.0) * hn, wml_ref[...])

        u = zc_ref[rows, :].astype(F32) * zx_ref[rows, :].astype(F32)
        u_prev = jnp.where(has_prev, pltpu.roll(u, 1, axis=0), 0.0)
        u_next = jnp.where(has_next, pltpu.roll(u, sub - 1, axis=0), 0.0)
        uc = u_prev * cw_ref[0:1] + u * cw_ref[1:2] + u_next * cw_ref[2:3] + cb_ref[...]
        sc = _bdot(zb_ref[rows, :] * uc.astype(BF16), wsc_ref[...])

        y = (zgm_ref[rows, :].astype(F32) + 1.0) * ml + (zgs_ref[rows, :].astype(F32) + 1.0) * sc
        x2_ref[rows, :] = x1_ref[rows, :] + gate * _bdot(y.astype(BF16), wo_ref[...])


def kernel(x_prompt, x_sample, state_C, state_n, state_m, c, c_ctx, w_mod, b_mod, norm_g,
           ffn1_w1, ffn1_w2, w_in, b_in, conv_w, conv_b, ml_norm, w_ml_out, w_sc_out, w_o,
           ffn2_w1, ffn2_w2, final_norm):
    assert w_mod.shape[0] == 1, "single trunk layer only"
    bp, tp, dm = x_prompt.shape
    bs, ts, _ = x_sample.shape
    n_ctx, n_lat = bp * tp, bs * ts
    n_tok = n_ctx + n_lat
    d_ff = ffn1_w2.shape[1]
    ml_dim = w_ml_out.shape[1]
    sc_dim = w_sc_out.shape[1]
    dh = ml_dim // NH
    n_gate = 2 * N_DIR * NH
    assert ml_dim == dm and sc_dim == dm and dh % MXU_TILE == 0 and dm == TN_IN
    assert w_in.shape[2] == 9 * dm + n_gate and n_gate < LANES
    assert n_ctx % TM_FFN == 0 and ts % TM_FFN == 0 and n_tok % TM_IN == 0
    assert n_ctx % TM_MIX == 0 and ts % TM_MIX == 0 and MIX_SUB % tp == 0 and MIX_SUB % GRID_W == 0
    assert TM_K % CHUNK == 0 and n_tok % TM_K == 0
    assert tp == CHUNK and ts % (2 * CHUNK) == 0 and bp % CTX_SEQS == 0 and n_ctx % ts == 0
    n_cond = 1 + bs

    cond = jnp.concatenate([c_ctx[None, :], c], axis=0)
    condT = jnp.pad(cond.T, ((0, 0), (0, SUBLANES - n_cond)))
    mod_spec = _resident((N_MOD, SUBLANES, dm))

    assert w_mod.shape[2] == N_MOD * dm and N_MOD <= FFN_WSTEPS
    w_in0, b_in0 = w_in[0], b_in.reshape(-1)
    gate_lo = 4 * ml_dim
    ncb = n_ctx // TM_FFN
    nb = n_tok // TM_FFN
    chunks = _ff_chunks(d_ff)
    wc1, wc2 = 2 * d_ff // FFN_WSTEPS, d_ff // FFN_WSTEPS
    assert wc1 * FFN_WSTEPS == 2 * d_ff and wc1 % LANES == 0
    assert wc2 * FFN_WSTEPS == d_ff and wc2 % (2 * SUBLANES) == 0

    def tok(i):
        return jnp.maximum(i - FFN_WSTEPS, 0)

    def wstep(i):
        return jnp.minimum(i, FFN_WSTEPS - 1)

    ffn_w_specs = [pl.BlockSpec((dm, wc1), lambda i: (0, wstep(i))),
                   pl.BlockSpec((wc2, dm), lambda i: (wstep(i), 0))]
    ffn_w_scratch = [pltpu.VMEM((dm, 2 * d_ff), BF16), pltpu.VMEM((d_ff, dm), BF16)]
    tok_spec = pl.BlockSpec((TM_FFN, dm), lambda i: (tok(i), 0))
    x1, h2, mod = pl.pallas_call(
        functools.partial(_ffn1_kernel, chunks=chunks, n_ctx_blocks=ncb,
                          blocks_per_seq=ts // TM_FFN, n_cond=n_cond),
        grid=(FFN_WSTEPS + nb,),
        in_specs=[pl.BlockSpec((TM_FFN, dm), lambda i: (jnp.minimum(tok(i), ncb - 1), 0)),
                  pl.BlockSpec((TM_FFN, dm), lambda i: (jnp.maximum(tok(i) - ncb, 0), 0)),
                  _resident((dm, SUBLANES)),
                  pl.BlockSpec((dm, dm), lambda i: (0, jnp.minimum(i, N_MOD - 1))),
                  _resident((N_MOD, dm)), _resident((3, dm))] + ffn_w_specs,
        out_specs=[tok_spec, tok_spec,
                   pl.BlockSpec((N_MOD, SUBLANES, dm), lambda i: (0, 0, 0))],
        out_shape=[jax.ShapeDtypeStruct((n_tok, dm), F32), jax.ShapeDtypeStruct((n_tok, dm), BF16),
                   jax.ShapeDtypeStruct((N_MOD, SUBLANES, dm), F32)],
        scratch_shapes=ffn_w_scratch,
        compiler_params=_params(1),
        name="ffn1",
    )(x_prompt.reshape(n_ctx, dm), x_sample.reshape(n_lat, dm), condT, w_mod[0],
      b_mod.reshape(N_MOD, dm), norm_g[0], ffn1_w1[0], ffn1_w2[0])

    w_in_t = w_in0.T
    n_lead = gate_lo // TN_IN - 1
    b_main = jnp.concatenate([b_in0[:ml_dim], b_in0[2 * ml_dim:gate_lo],
                              b_in0[gate_lo + n_gate:]])[None, :]
    n_main = b_main.shape[1]

    def w_block(j):
        return jnp.where(j == 0, 0, j + 1)

    z = pl.pallas_call(
        functools.partial(_inproj_kernel, n_lead=n_lead, shift=n_gate),
        grid=(n_main // TN_IN, n_tok // TM_IN),
        in_specs=[pl.BlockSpec((TM_IN, dm), lambda j, i: (i, 0)),
                  pl.BlockSpec((TN_IN, dm), lambda j, i: (w_block(j), 0)),
                  pl.BlockSpec((TN_IN, dm), lambda j, i: (w_block(j) + 1, 0)),
                  pl.BlockSpec((1, TN_IN), lambda j, i: (0, j))],
        out_specs=pl.BlockSpec((TM_IN, TN_IN), lambda j, i: (i, j)),
        out_shape=jax.ShapeDtypeStruct((n_tok, n_main), BF16),
        scratch_shapes=[pltpu.VMEM((TN_IN, dm), BF16)],
        compiler_params=_params(2),
        name="inproj",
    )(h2, w_in_t, w_in_t, b_main)

    kscale = dh ** -0.5
    assert ml_dim % dm == 0 and gate_lo % n_gate == 0
    b_kg = jnp.concatenate([b_in0[ml_dim:2 * ml_dim] * kscale,
                            b_in0[gate_lo:gate_lo + n_gate]])[:, None]
    kt, gates_t = pl.pallas_call(
        functools.partial(_kproj_kernel, k_scale=kscale),
        grid=(n_tok // TM_K,),
        in_specs=[pl.BlockSpec((TM_K, dm), lambda i: (i, 0)),
                  pl.BlockSpec((dm, dm), lambda i: (ml_dim // dm, 0),
                               pipeline_mode=pl.Buffered(1)),
                  pl.BlockSpec((n_gate, dm), lambda i: (gate_lo // n_gate, 0),
                               pipeline_mode=pl.Buffered(1)),
                  _resident((dm + n_gate, 1))],
        out_specs=[pl.BlockSpec((TM_K // CHUNK, dm, CHUNK), lambda i: (i, 0, 0)),
                   pl.BlockSpec((n_gate, TM_K), lambda i: (0, i))],
        out_shape=[jax.ShapeDtypeStruct((n_tok // CHUNK, dm, CHUNK), BF16),
                   jax.ShapeDtypeStruct((n_gate, n_tok), F32)],
        scratch_shapes=[pltpu.VMEM((dm + n_gate, dm), BF16)],
        compiler_params=_params(1),
        name="kproj",
    )(h2, w_in_t, w_in_t, b_kg)

    hn_ctx, new_c, new_n, new_m = _mlstm_call(
        z, kt, gates_t, ml_norm, None,
        n_seq=bp, t_len=tp, n_sub=CTX_SEQS, tok_block0=0, dh=dh, unroll=4)
    (hn_lat,) = _mlstm_call(
        z, kt, gates_t, ml_norm, (state_C, state_n, state_m),
        n_seq=bs, t_len=ts, n_sub=1, tok_block0=n_ctx // ts, dh=dh, unroll=4)

    ncb_m = n_ctx // TM_MIX
    tokm = pl.BlockSpec((TM_MIX, dm), lambda i: (i, 0))

    def zcol(j):
        return pl.BlockSpec((TM_MIX, dm), lambda i: (i, j))

    x2 = pl.pallas_call(
        functools.partial(_mixout_kernel, n_ctx_blocks=ncb_m, blocks_per_seq=ts // TM_MIX,
                          ctx_seg=tp),
        grid=(n_tok // TM_MIX,),
        in_specs=[pl.BlockSpec((TM_MIX, dm), lambda i: (jnp.minimum(i, ncb_m - 1), 0)),
                  pl.BlockSpec((TM_MIX, dm), lambda i: (jnp.maximum(i - ncb_m, 0), 0)),
                  zcol(ZO), zcol(ZB), zcol(ZC), zcol(ZX), zcol(ZGM), zcol(ZGS), tokm,
                  mod_spec, _resident((3, dm)), _resident((1, dm)),
                  _resident((dm, dm)), _resident((dm, dm)), _resident((dm, dm))],
        out_specs=tokm,
        out_shape=jax.ShapeDtypeStruct((n_tok, dm), F32),
        scratch_shapes=[pltpu.VMEM((dm, dm), BF16)] * 3,
        compiler_params=_params(1),
        name="mixout",
    )(hn_ctx, hn_lat, z, z, z, z, z, z, x1, mod, conv_w[0],
      conv_b[0][None, :], w_ml_out[0], w_sc_out[0], w_o[0])

    y_p, y_s = pl.pallas_call(
        functools.partial(_ffn2_kernel, chunks=chunks, n_ctx_blocks=ncb,
                          blocks_per_seq=ts // TM_FFN),
        grid=(FFN_WSTEPS + nb,),
        in_specs=[tok_spec, mod_spec, _resident((3, dm)), _resident((1, dm))]
        + ffn_w_specs,
        out_specs=[pl.BlockSpec((TM_FFN, dm), lambda i: (jnp.minimum(tok(i), ncb - 1), 0)),
                   pl.BlockSpec((TM_FFN, dm), lambda i: (jnp.maximum(tok(i) - ncb, 0), 0))],
        out_shape=[jax.ShapeDtypeStruct((n_ctx, dm), F32), jax.ShapeDtypeStruct((n_lat, dm), F32)],
        scratch_shapes=ffn_w_scratch,
        compiler_params=_params(1),
        name="ffn2",
    )(x2, mod, norm_g[0], final_norm[None, :], ffn2_w1[0], ffn2_w2[0])

    return (y_p.reshape(bp, tp, dm), y_s.reshape(bs, ts, dm), new_c,
            new_n, new_m[..., 0, 0].reshape(bp, 1, N_DIR, NH))
```
